```python
import math
import numpy as np
import jax, jax.numpy as jnp
from jax import lax

D_MODEL = 1024
BATCH = 2
SEQ = 16384
DEPTH = 2

N_MIXERS = 2
S5_GROUP = 16
S5_GROUPS = D_MODEL // S5_GROUP
S5_STATE = 64
HEAD_DIM = 64
N_HEADS = D_MODEL // HEAD_DIM
D_ATTN = N_HEADS * HEAD_DIM
DILATED_PAIRS = ((128, 1), (512, 4), (2048, 16))
N_DIL_GROUPS = len(DILATED_PAIRS)
ATTN_BLOCK = 128
N_BUCKETS = 32
MAX_DISTANCE = 2048
D_FF = ((8 * D_MODEL // 3 + 127) // 128) * 128
N_EXPERTS = 8
TOP_K = 2
D_FF_EXPERT = 7 * D_MODEL // 2
N_S5_LAYERS = (DEPTH + 1) // 2
N_ATTN_LAYERS = DEPTH // 2
DEEPNORM_ALPHA = (2 * DEPTH) ** 0.25
DEEPNORM_BETA = (8 * DEPTH) ** -0.25
LN_EPS = 1e-5
NEG_INF = -1e30

kernel_name = 'hybrid_s5_dilated_attn_moe_deepnorm'


def _layer_norm(x, g, b):
    xf = x.astype(jnp.float32)
    mu = jnp.mean(xf, axis=-1, keepdims=True)
    var = jnp.mean(jnp.square(xf - mu), axis=-1, keepdims=True)
    y = (xf - mu) * lax.rsqrt(var + LN_EPS)
    return (y * g.astype(jnp.float32) + b.astype(jnp.float32)).astype(x.dtype)


def _t5_bucket(dist):
    max_exact = N_BUCKETS // 2
    d = np.maximum(dist, 0)
    large = max_exact + (np.log(np.maximum(d, 1) / max_exact) / math.log(MAX_DISTANCE / max_exact)
                         * (N_BUCKETS - max_exact)).astype(np.int64)
    large = np.minimum(large, N_BUCKETS - 1)
    return np.where(d < max_exact, d, large).astype(np.int32)


def _s5_mixer(x, w_in, a_re, a_im, log_dt, b_re, b_im, c_re, c_im, d_skip, w_glu):
    f32 = jnp.float32
    bsz, seq, _ = x.shape
    u = (x @ w_in).reshape(bsz, seq, S5_GROUPS, S5_GROUP).astype(f32)
    dt = jnp.exp(log_dt.astype(f32))[:, None]
    lam_re = jnp.minimum(a_re.astype(f32), -1e-4)
    lam_im = a_im.astype(f32)
    mag = jnp.exp(lam_re * dt)
    ab_re = mag * jnp.cos(lam_im * dt)
    ab_im = mag * jnp.sin(lam_im * dt)
    den = lam_re * lam_re + lam_im * lam_im
    nr = ab_re - 1.0
    q_re = (nr * lam_re + ab_im * lam_im) / den
    q_im = (ab_im * lam_re - nr * lam_im) / den
    br, bi = b_re.astype(f32), b_im.astype(f32)
    bb_re = q_re[..., None] * br - q_im[..., None] * bi
    bb_im = q_re[..., None] * bi + q_im[..., None] * br
    bu_re = jnp.einsum('bsgc,gpc->bsgp', u, bb_re)
    bu_im = jnp.einsum('bsgc,gpc->bsgp', u, bb_im)
    a_re_t = jnp.broadcast_to(ab_re, bu_re.shape)
    a_im_t = jnp.broadcast_to(ab_im, bu_re.shape)

    def combine(e1, e2):
        a1r, a1i, b1r, b1i = e1
        a2r, a2i, b2r, b2i = e2
        return (a2r * a1r - a2i * a1i,
                a2r * a1i + a2i * a1r,
                a2r * b1r - a2i * b1i + b2r,
                a2r * b1i + a2i * b1r + b2i)

    _, _, h_re, h_im = lax.associative_scan(combine, (a_re_t, a_im_t, bu_re, bu_im), axis=1)
    y = (jnp.einsum('bsgp,gcp->bsgc', h_re, c_re.astype(f32))
         - jnp.einsum('bsgp,gcp->bsgc', h_im, c_im.astype(f32))
         + d_skip.astype(f32).reshape(S5_GROUPS, S5_GROUP) * u)
    y = y.reshape(bsz, seq, D_MODEL).astype(x.dtype)
    g = jax.nn.gelu(y)
    vg = g @ w_glu
    return vg[..., :D_MODEL] * jax.nn.sigmoid(vg[..., D_MODEL:])


def _dilated_branch(q, k, v, rel_bias, window, dilation):
    f32 = jnp.float32
    bsz, seq, nh, hd = q.shape
    span = window // dilation
    L = ATTN_BLOCK
    chunk = dilation * L
    sp = -(-seq // chunk) * chunk
    nb = sp // chunk

    def to_blocks(t):
        t = jnp.pad(t, ((0, 0), (0, sp - seq), (0, 0), (0, 0)))
        t = t.reshape(bsz, sp // dilation, dilation, nh, t.shape[-1]).transpose(0, 2, 1, 3, 4)
        return t.reshape(bsz, dilation, nb, L, nh, t.shape[-1])

    def from_blocks(t):
        t = t.reshape(bsz, dilation, sp // dilation, nh, t.shape[-1]).transpose(0, 2, 1, 3, 4)
        return t.reshape(bsz, sp, nh, t.shape[-1])[:, :seq]

    qb, kb, vb = to_blocks(q), to_blocks(k), to_blocks(v)
    def with_prev(t):
        prev = jnp.concatenate([jnp.zeros_like(t[:, :, :1]), t[:, :, :-1]], axis=2)
        return jnp.concatenate([prev, t], axis=3)
    kk, vv = with_prev(kb), with_prev(vb)

    qi = np.arange(L)[:, None]
    ki = np.arange(2 * L)[None, :]
    off = qi + L - ki
    band = (off >= 0) & (off <= span)
    mask = band[None] & ((np.arange(nb)[:, None, None] > 0) | (ki >= L)[None])
    bucket = _t5_bucket(np.clip(off, 0, None) * dilation)
    bias = jnp.transpose(rel_bias[bucket], (2, 0, 1)).astype(f32)

    logits = jnp.einsum('brnqhd,brnkhd->brnhqk', qb, kk).astype(f32) * (1.0 / math.sqrt(hd)) + bias
    logits = jnp.where(mask[None, None, :, None], logits, NEG_INF)
    m = jnp.max(logits, axis=-1, keepdims=True)
    p = jnp.exp(logits - m)
    s = jnp.sum(p, axis=-1, keepdims=True)
    o = jnp.einsum('brnhqk,brnkhd->brnqhd', p, vv.astype(f32)) / jnp.swapaxes(s, 3, 4)
    lse = jnp.swapaxes(m + jnp.log(s), 3, 4)
    return from_blocks(o), from_blocks(lse)[..., 0]


def _dilated_attention(x, w_qkv, w_o, rel_bias):
    bsz, seq, _ = x.shape
    qkv = (x @ w_qkv).reshape(bsz, seq, N_DIL_GROUPS, 3, N_HEADS, HEAD_DIM)
    outs, lses = [], []
    for g, (window, dilation) in enumerate(DILATED_PAIRS):
        o, lse = _dilated_branch(qkv[:, :, g, 0], qkv[:, :, g, 1], qkv[:, :, g, 2], rel_bias, window, dilation)
        outs.append(o)
        lses.append(lse)
    w = jax.nn.softmax(jnp.stack(lses, axis=0), axis=0)
    merged = jnp.einsum('gbsh,gbshd->bshd', w, jnp.stack(outs, axis=0))
    return merged.reshape(bsz, seq, D_ATTN).astype(x.dtype) @ w_o


def _swiglu(x, w1, w3, w2):
    return (jax.nn.silu(x @ w1) * (x @ w3)) @ w2


def _moe(x, w_router, w1, w3, w2):
    logits = (x @ w_router).astype(jnp.float32)
    top_v, top_i = lax.top_k(logits, TOP_K)
    gates = jax.nn.softmax(top_v, axis=-1)
    gate_full = jnp.sum(jax.nn.one_hot(top_i, N_EXPERTS, dtype=jnp.float32) * gates[..., None], axis=-2)
    out = jnp.zeros_like(x)
    for e in range(N_EXPERTS):
        y = _swiglu(x, w1[e], w3[e], w2[e])
        out = out + gate_full[..., e, None].astype(x.dtype) * y
    return out


def setup_inputs(seed: int = 0) -> dict:
    key = jax.random.key(seed)
    ks = jax.random.split(key, 24)
    f32 = jnp.float32

    def nrm(k, shape, scale):
        return jax.random.normal(k, shape, f32) * scale

    NS, NA = N_S5_LAYERS, N_ATTN_LAYERS
    x = nrm(ks[0], (BATCH, SEQ, D_MODEL), 1.0)
    s5_w_in = nrm(ks[1], (NS, D_MODEL, D_MODEL), D_MODEL ** -0.5)
    s5_a_re = -0.5 + nrm(ks[2], (NS, S5_GROUPS, S5_STATE), 0.01)
    s5_a_im = math.pi * jnp.arange(S5_STATE, dtype=f32) + nrm(ks[3], (NS, S5_GROUPS, S5_STATE), 0.01)
    s5_log_dt = jax.random.uniform(ks[4], (NS, S5_GROUPS), f32, math.log(1e-3), math.log(1e-1))
    s5_b_re = nrm(ks[5], (NS, S5_GROUPS, S5_STATE, S5_GROUP), (2 * S5_GROUP) ** -0.5)
    s5_b_im = nrm(ks[6], (NS, S5_GROUPS, S5_STATE, S5_GROUP), (2 * S5_GROUP) ** -0.5)
    s5_c_re = nrm(ks[7], (NS, S5_GROUPS, S5_GROUP, S5_STATE), (2 * S5_STATE) ** -0.5)
    s5_c_im = nrm(ks[8], (NS, S5_GROUPS, S5_GROUP, S5_STATE), (2 * S5_STATE) ** -0.5)
    s5_d = nrm(ks[9], (NS, D_MODEL), 1.0)
    glu_scale = jnp.concatenate([jnp.full((D_MODEL,), DEEPNORM_BETA, f32), jnp.ones((D_MODEL,), f32)])
    s5_w_glu = nrm(ks[10], (NS, D_MODEL, 2 * D_MODEL), D_MODEL ** -0.5) * glu_scale
    qkv_scale = jnp.array([1.0, 1.0, DEEPNORM_BETA], f32)[:, None]
    attn_w_qkv = (nrm(ks[11], (NA, D_MODEL, N_DIL_GROUPS, 3, D_ATTN), D_MODEL ** -0.5)
                  * qkv_scale).reshape(NA, D_MODEL, N_DIL_GROUPS * 3 * D_ATTN)
    attn_w_o = nrm(ks[12], (NA, D_ATTN, D_MODEL), D_ATTN ** -0.5 * DEEPNORM_BETA)
    rel_bias = nrm(ks[13], (N_BUCKETS, N_HEADS), 0.5)
    ffn_w1 = nrm(ks[14], (NS, D_MODEL, D_FF), D_MODEL ** -0.5)
    ffn_w3 = nrm(ks[15], (NS, D_MODEL, D_FF), D_MODEL ** -0.5)
    ffn_w2 = nrm(ks[16], (NS, D_FF, D_MODEL), D_FF ** -0.5 * DEEPNORM_BETA)
    moe_w_router = nrm(ks[17], (NA, D_MODEL, N_EXPERTS), D_MODEL ** -0.5)
    moe_w1 = nrm(ks[18], (NA, N_EXPERTS, D_MODEL, D_FF_EXPERT), D_MODEL ** -0.5)
    moe_w3 = nrm(ks[19], (NA, N_EXPERTS, D_MODEL, D_FF_EXPERT), D_MODEL ** -0.5)
    moe_w2 = nrm(ks[20], (NA, N_EXPERTS, D_FF_EXPERT, D_MODEL), D_FF_EXPERT ** -0.5 * DEEPNORM_BETA)
    ln_g = 1.0 + nrm(ks[21], (DEPTH, 2, D_MODEL), 0.02)
    ln_b = nrm(ks[22], (DEPTH, 2, D_MODEL), 0.02)
    return {'x': x, 's5_w_in': s5_w_in, 's5_a_re': s5_a_re, 's5_a_im': s5_a_im, 's5_log_dt': s5_log_dt,
            's5_b_re': s5_b_re, 's5_b_im': s5_b_im, 's5_c_re': s5_c_re, 's5_c_im': s5_c_im, 's5_d': s5_d,
            's5_w_glu': s5_w_glu, 'attn_w_qkv': attn_w_qkv, 'attn_w_o': attn_w_o, 'rel_bias': rel_bias,
            'ffn_w1': ffn_w1, 'ffn_w3': ffn_w3, 'ffn_w2': ffn_w2, 'moe_w_router': moe_w_router,
            'moe_w1': moe_w1, 'moe_w3': moe_w3, 'moe_w2': moe_w2, 'ln_g': ln_g, 'ln_b': ln_b}


def reference(x, s5_w_in, s5_a_re, s5_a_im, s5_log_dt, s5_b_re, s5_b_im, s5_c_re, s5_c_im, s5_d,
              s5_w_glu, attn_w_qkv, attn_w_o, rel_bias, ffn_w1, ffn_w3, ffn_w2, moe_w_router,
              moe_w1, moe_w3, moe_w2, ln_g, ln_b):
    for i in range(DEPTH):
        j = i // N_MIXERS
        if i % N_MIXERS == 0:
            h = _s5_mixer(x, s5_w_in[j], s5_a_re[j], s5_a_im[j], s5_log_dt[j], s5_b_re[j], s5_b_im[j],
                          s5_c_re[j], s5_c_im[j], s5_d[j], s5_w_glu[j])
            x = _layer_norm(DEEPNORM_ALPHA * x + h, ln_g[i, 0], ln_b[i, 0])
            f = _swiglu(x, ffn_w1[j], ffn_w3[j], ffn_w2[j])
        else:
            h = _dilated_attention(x, attn_w_qkv[j], attn_w_o[j], rel_bias)
            x = _layer_norm(DEEPNORM_ALPHA * x + h, ln_g[i, 0], ln_b[i, 0])
            f = _moe(x, moe_w_router[j], moe_w1[j], moe_w3[j], moe_w2[j])
        x = _layer_norm(DEEPNORM_ALPHA * x + f, ln_g[i, 1], ln_b[i, 1])
    return x
```

```python
import functools
import math

import numpy as np
import jax
import jax.numpy as jnp
from jax import lax
from jax.experimental import pallas as pl
from jax.experimental.pallas import tpu as pltpu

D_MODEL = 1024
S5_GROUP = 16
S5_GROUPS = D_MODEL // S5_GROUP
S5_STATE = 64
S5_WIDTH = S5_GROUPS * S5_STATE
HEAD_DIM = 64
N_HEADS = D_MODEL // HEAD_DIM
DILATED_PAIRS = ((128, 1), (512, 4), (2048, 16))
N_DIL_GROUPS = len(DILATED_PAIRS)
ATTN_BLOCK = 128
N_BUCKETS = 32
MAX_DISTANCE = 2048
N_EXPERTS = 8
TOP_K = 2
DEPTH = 2
DEEPNORM_ALPHA = (2 * DEPTH) ** 0.25
LN_EPS = 1e-5
NEG_INF = -1e30

LANE = 128
SUBLANE = 8
MXU_DIM = 256
VMEM_LIMIT = 56 * 1024 * 1024

S5_T = 32
S5_TB = SUBLANE * S5_T
S5_LC = 512
S5_KT = MXU_DIM
S5_NT = D_MODEL // S5_KT
S5_ST = S5_WIDTH // S5_NT
FFN_TM = 512
QKV_TM = 1024
ATT_QB = 4
MRG_TM = 512
MOE_TM = 512
MOE_TF = 1792
GATHER_TM = 512
COMBINE_TM = 256

f32 = jnp.float32
bf16 = jnp.bfloat16


def _cparams(sem):
    return pltpu.CompilerParams(dimension_semantics=sem, vmem_limit_bytes=VMEM_LIMIT)


def _const_spec(shape):
    nd = len(shape)
    return pl.BlockSpec(shape, lambda *_: (0,) * nd, pipeline_mode=pl.Buffered(1))


def _layer_norm_rows(v, g, b):
    mu = jnp.mean(v, axis=-1, keepdims=True)
    c = v - mu
    var = jnp.mean(c * c, axis=-1, keepdims=True)
    return c * lax.rsqrt(var + LN_EPS) * g + b


def _gelu_tanh(x):
    return 0.5 * x * (1.0 + jnp.tanh(math.sqrt(2.0 / math.pi) * (x + 0.044715 * (x * x * x))))


def _dot(a, b):
    return jnp.dot(a, b, preferred_element_type=f32)


_TAB_A = 0
_TAB_P1 = 2
_TAB_P2 = 4
_TAB_P4 = 6
_TAB_PC_RE = 8
_TAB_PC_IM = 16
_TAB_ROWS = 24


def _s5_kernel(x_ref, win_ref, bre_ref, bim_ref, cre_ref, cimn_ref, tab_ref, d_ref, wglu_ref,
               g_ref, b_ref, o_ref, u_ref, hre_ref, him_ref, car_re, car_im):
    T = S5_T
    x = x_ref[...]
    u_ref[...] = _dot(x.astype(bf16), win_ref[...])

    for kt in range(S5_NT):
        ub = u_ref[:, kt * S5_KT:(kt + 1) * S5_KT].astype(bf16)
        hre_ref[:, kt * S5_ST:(kt + 1) * S5_ST] = _dot(ub, bre_ref[kt])
        him_ref[:, kt * S5_ST:(kt + 1) * S5_ST] = _dot(ub, bim_ref[kt])

    @pl.when(pl.program_id(1) == 0)
    def _():
        car_re[...] = jnp.zeros_like(car_re)
        car_im[...] = jnp.zeros_like(car_im)

    row = lax.broadcasted_iota(jnp.int32, (SUBLANE, S5_LC), 0)

    def bcast(r, sl):
        return jnp.broadcast_to(tab_ref[r:r + 1, sl], (SUBLANE, S5_LC))

    def cmul(ar, ai, xr, xi):
        return ar * xr - ai * xi, ar * xi + ai * xr

    for lc in range(S5_WIDTH // S5_LC):
        sl = slice(lc * S5_LC, (lc + 1) * S5_LC)
        ar, ai = bcast(_TAB_A, sl), bcast(_TAB_A + 1, sl)

        def scan_body(t, carry):
            hr, hi = carry
            r0 = pl.multiple_of(t * SUBLANE, SUBLANE)
            pr, pi = cmul(ar, ai, hr, hi)
            nr = pr + hre_ref[pl.ds(r0, SUBLANE), sl]
            ni = pi + him_ref[pl.ds(r0, SUBLANE), sl]
            hre_ref[pl.ds(r0, SUBLANE), sl] = nr
            him_ref[pl.ds(r0, SUBLANE), sl] = ni
            return nr, ni

        zero = jnp.zeros((SUBLANE, S5_LC), f32)
        er, ei = lax.fori_loop(0, T, scan_body, (zero, zero), unroll=2)

        xr, xi = er, ei
        for s, tr in ((1, _TAB_P1), (2, _TAB_P2), (4, _TAB_P4)):
            sr = jnp.where(row >= s, pltpu.roll(xr, s, 0), 0.0)
            si = jnp.where(row >= s, pltpu.roll(xi, s, 0), 0.0)
            mr, mi = cmul(bcast(tr, sl), bcast(tr + 1, sl), sr, si)
            xr, xi = xr + mr, xi + mi
        shr = jnp.where(row >= 1, pltpu.roll(xr, 1, 0), 0.0)
        shi = jnp.where(row >= 1, pltpu.roll(xi, 1, 0), 0.0)
        cr, ci = car_re[:, sl], car_im[:, sl]
        pcr = tab_ref[_TAB_PC_RE:_TAB_PC_RE + SUBLANE, sl]
        pci = tab_ref[_TAB_PC_IM:_TAB_PC_IM + SUBLANE, sl]
        hr0, hi0 = cmul(pcr, pci, cr, ci)
        hr0, hi0 = hr0 + shr, hi0 + shi
        nr, ni = cmul(bcast(_TAB_P1, sl), bcast(_TAB_P1 + 1, sl), hr0, hi0)
        nr, ni = nr + er, ni + ei
        car_re[:, sl] = jnp.broadcast_to(nr[SUBLANE - 1:SUBLANE], (SUBLANE, S5_LC))
        car_im[:, sl] = jnp.broadcast_to(ni[SUBLANE - 1:SUBLANE], (SUBLANE, S5_LC))

        def fix_body(t, carry):
            zr, zi = carry
            r0 = pl.multiple_of(t * SUBLANE, SUBLANE)
            zr, zi = cmul(ar, ai, zr, zi)
            hre_ref[pl.ds(r0, SUBLANE), sl] = hre_ref[pl.ds(r0, SUBLANE), sl] + zr
            him_ref[pl.ds(r0, SUBLANE), sl] = him_ref[pl.ds(r0, SUBLANE), sl] + zi
            return zr, zi

        lax.fori_loop(0, T, fix_body, (hr0, hi0), unroll=2)

    ys = []
    for nt in range(S5_NT):
        hr = hre_ref[:, nt * S5_ST:(nt + 1) * S5_ST].astype(bf16)
        hi = him_ref[:, nt * S5_ST:(nt + 1) * S5_ST].astype(bf16)
        csl = slice(nt * S5_KT, (nt + 1) * S5_KT)
        ys.append(_dot(hr, cre_ref[nt]) + _dot(hi, cimn_ref[nt]) + d_ref[:, csl] * u_ref[:, csl])
    y = jnp.concatenate(ys, axis=1)
    vg = _dot(_gelu_tanh(y).astype(bf16), wglu_ref[...])
    h = vg[:, :D_MODEL] * jax.nn.sigmoid(vg[:, D_MODEL:])
    o_ref[...] = _layer_norm_rows(DEEPNORM_ALPHA * x + h, g_ref[...], b_ref[...])


def _s5_tables(a_re, a_im, log_dt, b_re, b_im, c_re, c_im):
    dt = jnp.exp(log_dt.astype(f32))[:, None]
    lam_re = jnp.minimum(a_re.astype(f32), -1e-4)
    lam_im = a_im.astype(f32)
    mag = jnp.exp(lam_re * dt)
    ab_re = mag * jnp.cos(lam_im * dt)
    ab_im = mag * jnp.sin(lam_im * dt)
    den = lam_re * lam_re + lam_im * lam_im
    nr = ab_re - 1.0
    q_re = (nr * lam_re + ab_im * lam_im) / den
    q_im = (ab_im * lam_re - nr * lam_im) / den
    br, bi = b_re.astype(f32), b_im.astype(f32)
    bb_re = q_re[..., None] * br - q_im[..., None] * bi
    bb_im = q_re[..., None] * bi + q_im[..., None] * br

    gpt = S5_KT // S5_GROUP
    eye = jnp.eye(gpt, dtype=f32)

    def b_blocks(bb):
        t = bb.reshape(S5_NT, gpt, S5_STATE, S5_GROUP)
        blk = jnp.einsum('ngpc,gh->ngchp', t, eye)
        return blk.reshape(S5_NT, S5_KT, S5_ST).astype(bf16)

    def c_blocks(cc):
        t = cc.reshape(S5_NT, gpt, S5_GROUP, S5_STATE)
        blk = jnp.einsum('ngcp,gh->ngphc', t, eye)
        return blk.reshape(S5_NT, S5_ST, S5_KT).astype(bf16)

    def csq(z):
        return z[0] * z[0] - z[1] * z[1], 2.0 * z[0] * z[1]

    def cmul(x, y):
        return x[0] * y[0] - x[1] * y[1], x[0] * y[1] + x[1] * y[0]

    a1 = (ab_re.reshape(-1), ab_im.reshape(-1))
    p = a1
    for _ in range(int(math.log2(S5_T))):
        p = csq(p)
    p1 = p
    p2 = csq(p1)
    p4 = csq(p2)
    pc = [(jnp.ones_like(p1[0]), jnp.zeros_like(p1[0]))]
    for _ in range(SUBLANE - 1):
        pc.append(cmul(pc[-1], p1))
    rows = [a1[0], a1[1], p1[0], p1[1], p2[0], p2[1], p4[0], p4[1]]
    rows += [z[0] for z in pc] + [z[1] for z in pc]
    tab = jnp.stack(rows, axis=0)
    return (b_blocks(bb_re), b_blocks(bb_im), c_blocks(c_re.astype(f32)), c_blocks(-c_im.astype(f32)), tab)


def _s5_layer(xp, w_in, bre, bim, cre, cimn, tab, d_skip, w_glu, ln_g, ln_b, batch):
    n = xp.shape[0]
    steps = n // batch // S5_TB
    row_spec = pl.BlockSpec((S5_TB, D_MODEL), lambda b, t: (b * steps + t, 0))
    return pl.pallas_call(
        _s5_kernel,
        grid=(batch, steps),
        in_specs=[row_spec,
                  _const_spec((D_MODEL, D_MODEL)),
                  _const_spec((S5_NT, S5_KT, S5_ST)), _const_spec((S5_NT, S5_KT, S5_ST)),
                  _const_spec((S5_NT, S5_ST, S5_KT)), _const_spec((S5_NT, S5_ST, S5_KT)),
                  _const_spec((_TAB_ROWS, S5_WIDTH)),
                  _const_spec((1, D_MODEL)),
                  _const_spec((D_MODEL, 2 * D_MODEL)),
                  _const_spec((1, D_MODEL)), _const_spec((1, D_MODEL))],
        out_specs=row_spec,
        out_shape=jax.ShapeDtypeStruct((n, D_MODEL), f32),
        scratch_shapes=[pltpu.VMEM((S5_TB, D_MODEL), f32),
                        pltpu.VMEM((S5_TB, S5_WIDTH), f32), pltpu.VMEM((S5_TB, S5_WIDTH), f32),
                        pltpu.VMEM((SUBLANE, S5_WIDTH), f32), pltpu.VMEM((SUBLANE, S5_WIDTH), f32)],
        compiler_params=_cparams(("arbitrary", "arbitrary")),
        name="s5_mixer",
    )(xp, w_in, bre, bim, cre, cimn, tab, d_skip, w_glu, ln_g, ln_b)


def _ffn_kernel(x_ref, w1_ref, w3_ref, w2_ref, g_ref, b_ref, o_ref):
    x = x_ref[...]
    xb = x.astype(bf16)
    h = jax.nn.silu(_dot(xb, w1_ref[...])) * _dot(xb, w3_ref[...])
    f = _dot(h.astype(bf16), w2_ref[...])
    o_ref[...] = _layer_norm_rows(DEEPNORM_ALPHA * x + f, g_ref[...], b_ref[...])


def _ffn_layer(x, w1, w3, w2, ln_g, ln_b):
    n = x.shape[0]
    d_ff = w1.shape[1]
    row_spec = pl.BlockSpec((FFN_TM, D_MODEL), lambda i: (i, 0))
    return pl.pallas_call(
        _ffn_kernel,
        grid=(n // FFN_TM,),
        in_specs=[row_spec, _const_spec((D_MODEL, d_ff)), _const_spec((D_MODEL, d_ff)),
                  _const_spec((d_ff, D_MODEL)), _const_spec((1, D_MODEL)), _const_spec((1, D_MODEL))],
        out_specs=row_spec,
        out_shape=jax.ShapeDtypeStruct((n, D_MODEL), f32),
        compiler_params=_cparams(("arbitrary",)),
        name="dense_ffn",
    )(x, w1, w3, w2, ln_g, ln_b)


def _qkv_kernel(x_ref, w_ref, o_ref):
    scale = jnp.where(pl.program_id(2) == 0, 1.0 / math.sqrt(HEAD_DIM), 1.0).astype(f32)
    o_ref[0] = (_dot(x_ref[0], w_ref[0]) * scale).astype(o_ref.dtype)


def _qkv_proj(xg, wg):
    ng, n, _ = xg.shape
    return pl.pallas_call(
        _qkv_kernel,
        grid=(ng, n // QKV_TM, 3),
        in_specs=[pl.BlockSpec((1, QKV_TM, D_MODEL), lambda g, i, j: (g, i, 0)),
                  pl.BlockSpec((1, D_MODEL, D_MODEL), lambda g, i, j: (g, 0, j))],
        out_specs=pl.BlockSpec((1, QKV_TM, D_MODEL), lambda g, i, j: (g, i, j)),
        out_shape=jax.ShapeDtypeStruct((ng, n, 3 * D_MODEL), bf16),
        compiler_params=_cparams(("arbitrary", "arbitrary", "arbitrary")),
        name="qkv_proj",
    )(xg, wg)


def _t5_bucket(dist):
    max_exact = N_BUCKETS // 2
    d = np.maximum(dist, 0)
    large = max_exact + (np.log(np.maximum(d, 1) / max_exact) / math.log(MAX_DISTANCE / max_exact)
                         * (N_BUCKETS - max_exact)).astype(np.int64)
    large = np.minimum(large, N_BUCKETS - 1)
    return np.where(d < max_exact, d, large).astype(np.int32)


def _attn_bias(rel_bias):
    L = ATTN_BLOCK
    qi = np.arange(L)[:, None]
    ki = np.arange(2 * L)[None, :]
    off = qi + L - ki
    out = []
    for window, dilation in DILATED_PAIRS:
        band = (off >= 0) & (off <= window // dilation)
        bucket = _t5_bucket(np.clip(off, 0, None) * dilation)
        bias = jnp.transpose(rel_bias[bucket], (2, 0, 1)).astype(f32)
        out.append(jnp.where(band[None], bias, NEG_INF))
    return jnp.stack(out, axis=0)


def _attn_kernel(nb_ref, q_ref, kp_ref, kc_ref, vp_ref, vc_ref, bias_ref, o_ref, lse_ref):
    L = ATTN_BLOCK
    g, i = pl.program_id(0), pl.program_id(1)
    lane = lax.broadcasted_iota(jnp.int32, (L, LANE), 1)
    low = lane < HEAD_DIM
    is_first = lax.rem(i * ATT_QB, nb_ref[g]) == 0
    col = lax.broadcasted_iota(jnp.int32, (L, 2 * L), 1)
    pen = jnp.where((col < L) & is_first, NEG_INF, 0.0).astype(f32)
    for s in range(ATT_QB):
        rs = slice(s * L, (s + 1) * L)
        q = q_ref[0, rs, :]
        if s == 0:
            k = jnp.concatenate([kp_ref[0], kc_ref[0, rs, :]], axis=0)
            v = jnp.concatenate([vp_ref[0], vc_ref[0, rs, :]], axis=0)
        else:
            ks = slice((s - 1) * L, (s + 1) * L)
            k = kc_ref[0, ks, :]
            v = vc_ref[0, ks, :]
        lse_acc = jnp.zeros((L, LANE), f32)
        for hp in range(N_HEADS // 2):
            cs = slice(hp * LANE, (hp + 1) * LANE)
            qp, kpair, vpair = q[:, cs], k[:, cs], v[:, cs]
            outs = []
            for half in range(2):
                h = 2 * hp + half
                qm = jnp.where(low if half == 0 else jnp.logical_not(low), qp, jnp.zeros_like(qp))
                logits = lax.dot_general(qm, kpair, (((1,), (1,)), ((), ())),
                                         preferred_element_type=f32)
                logits = logits + bias_ref[0, h]
                if s == 0:
                    logits = logits + pen
                m = jnp.max(logits, axis=-1, keepdims=True)
                p = jnp.exp(logits - m)
                ssum = jnp.sum(p, axis=-1, keepdims=True)
                outs.append(_dot(p.astype(bf16), vpair) / ssum)
                lse_acc = jnp.where(lane == h, m + jnp.log(ssum), lse_acc)
            o_ref[0, rs, cs] = jnp.where(low, outs[0], outs[1]).astype(o_ref.dtype)
        lse_ref[0, rs, :] = lse_acc


def _attention(qkv, bias, blocks_per_seq):
    ng, n, _ = qkv.shape
    L = ATTN_BLOCK
    tq = ATT_QB * L

    def cur(col):
        return pl.BlockSpec((1, tq, D_MODEL), lambda g, i, nb: (g, i, col))

    def prev(col):
        return pl.BlockSpec((1, L, D_MODEL), lambda g, i, nb: (g, jnp.maximum(i * ATT_QB - 1, 0), col))

    grid_spec = pltpu.PrefetchScalarGridSpec(
        num_scalar_prefetch=1,
        grid=(ng, n // tq),
        in_specs=[cur(0), prev(1), cur(1), prev(2), cur(2),
                  pl.BlockSpec((1, N_HEADS, L, 2 * L), lambda g, i, nb: (g, 0, 0, 0))],
        out_specs=[pl.BlockSpec((1, tq, D_MODEL), lambda g, i, nb: (g, i, 0)),
                   pl.BlockSpec((1, tq, LANE), lambda g, i, nb: (g, i, 0))],
    )
    return pl.pallas_call(
        _attn_kernel,
        grid_spec=grid_spec,
        out_shape=[jax.ShapeDtypeStruct((ng, n, D_MODEL), bf16),
                   jax.ShapeDtypeStruct((ng, n, LANE), f32)],
        compiler_params=_cparams(("arbitrary", "arbitrary")),
        name="dilated_attn",
    )(jnp.asarray(blocks_per_seq, jnp.int32), qkv, qkv, qkv, qkv, qkv, bias)


def _split_bf16(v):
    hi = v.astype(bf16)
    return hi, (v - hi.astype(f32)).astype(bf16)


def _merge_kernel(o_ref, lse_ref, x_ref, e_ref, wo_ref, g_ref, b_ref, wrh_ref, wrl_ref,
                  x3_ref, gate_ref, sel_ref):
    ls = [lse_ref[g] for g in range(N_DIL_GROUPS)]
    mx = functools.reduce(jnp.maximum, ls)
    es = [jnp.exp(l - mx) for l in ls]
    den = functools.reduce(jnp.add, es)
    merged = None
    for g in range(N_DIL_GROUPS):
        wh, wl = _split_bf16(es[g] / den)
        wfull = _dot(wh, e_ref[...]) + _dot(wl, e_ref[...])
        term = wfull * o_ref[g].astype(f32)
        merged = term if merged is None else merged + term
    att = _dot(merged.astype(bf16), wo_ref[...])
    x3 = _layer_norm_rows(DEEPNORM_ALPHA * x_ref[...] + att, g_ref[...], b_ref[...])
    x3_ref[...] = x3

    xh, xl = _split_bf16(x3)
    logits = _dot(xh, wrh_ref[...]) + _dot(xl, wrh_ref[...]) + _dot(xh, wrl_ref[...])
    lane = lax.broadcasted_iota(jnp.int32, logits.shape, 1)
    neg = -jnp.inf
    logits = jnp.where(lane < N_EXPERTS, logits, neg)
    m1 = jnp.max(logits, axis=-1, keepdims=True)
    i1 = jnp.min(jnp.where(logits == m1, lane, LANE), axis=-1, keepdims=True)
    pick1 = lane == i1
    rest = jnp.where(pick1, neg, logits)
    m2 = jnp.max(rest, axis=-1, keepdims=True)
    i2 = jnp.min(jnp.where(rest == m2, lane, LANE), axis=-1, keepdims=True)
    pick2 = lane == i2
    e2 = jnp.exp(m2 - m1)
    gate_ref[...] = jnp.where(pick1, 1.0 / (1.0 + e2), jnp.where(pick2, e2 / (1.0 + e2), 0.0))
    sel_ref[...] = jnp.where(pick1 | pick2, 1.0, 0.0)


def _merge_layer(o, lse, x, expand, w_o, ln_g, ln_b, wr_hi, wr_lo):
    n = x.shape[0]
    row = lambda w: pl.BlockSpec((MRG_TM, w), lambda i: (i, 0))
    grp = lambda w: pl.BlockSpec((N_DIL_GROUPS, MRG_TM, w), lambda i: (0, i, 0))
    return pl.pallas_call(
        _merge_kernel,
        grid=(n // MRG_TM,),
        in_specs=[grp(D_MODEL), grp(LANE), row(D_MODEL), _const_spec((LANE, D_MODEL)),
                  _const_spec((D_MODEL, D_MODEL)), _const_spec((1, D_MODEL)), _const_spec((1, D_MODEL)),
                  _const_spec((D_MODEL, LANE)), _const_spec((D_MODEL, LANE))],
        out_specs=[row(D_MODEL), row(LANE), row(LANE)],
        out_shape=[jax.ShapeDtypeStruct((n, D_MODEL), f32), jax.ShapeDtypeStruct((n, LANE), f32),
                   jax.ShapeDtypeStruct((n, LANE), f32)],
        compiler_params=_cparams(("arbitrary",)),
        name="attn_merge",
    )(o, lse, x, expand, w_o, ln_g, ln_b, wr_hi, wr_lo)


def _row_copy(src_hbm, src_row, dst_ref, dst_row, sem):
    return pltpu.make_async_copy(src_hbm.at[pl.ds(src_row, 1), :], dst_ref.at[pl.ds(dst_row, 1), :], sem)


def _gather_kernel(idx_ref, x_hbm, o_ref, sem):
    def issue(r, carry):
        _row_copy(x_hbm, idx_ref[0, 0, r], o_ref, r, sem).start()
        return carry

    lax.fori_loop(0, GATHER_TM, issue, 0)

    def drain(r, carry):
        _row_copy(x_hbm, 0, o_ref, r, sem).wait()
        return carry

    lax.fori_loop(0, GATHER_TM, drain, 0)


def _gather_rows(idx, x):
    n_tiles = idx.shape[0]
    return pl.pallas_call(
        _gather_kernel,
        grid=(n_tiles,),
        in_specs=[pl.BlockSpec((1, 1, GATHER_TM), lambda i: (i, 0, 0), memory_space=pltpu.SMEM),
                  pl.BlockSpec(memory_space=pl.ANY)],
        out_specs=pl.BlockSpec((GATHER_TM, D_MODEL), lambda i: (i, 0)),
        out_shape=jax.ShapeDtypeStruct((n_tiles * GATHER_TM, D_MODEL), x.dtype),
        scratch_shapes=[pltpu.SemaphoreType.DMA(())],
        compiler_params=_cparams(("arbitrary",)),
        name="moe_gather",
    )(idx, x)


def _moe_kernel(te_ref, act_ref, x_ref, w1_ref, w3_ref, w2_ref, o_ref):
    del te_ref
    i, f = pl.program_id(0), pl.program_id(1)

    @pl.when(act_ref[i] == 1)
    def _():
        xb = x_ref[...].astype(bf16)
        h = jax.nn.silu(_dot(xb, w1_ref[0])) * _dot(xb, w3_ref[0])
        part = _dot(h.astype(bf16), w2_ref[0])

        @pl.when(f == 0)
        def _():
            o_ref[...] = part

        @pl.when(f > 0)
        def _():
            o_ref[...] = o_ref[...] + part

    @pl.when((act_ref[i] == 0) & (f == 0))
    def _():
        o_ref[...] = jnp.zeros_like(o_ref)


def _moe_gmm(tile_expert, tile_active, xs, w1, w3, w2):
    n_tiles = tile_expert.shape[0]
    d_ff = w1.shape[2]
    grid_spec = pltpu.PrefetchScalarGridSpec(
        num_scalar_prefetch=2,
        grid=(n_tiles, d_ff // MOE_TF),
        in_specs=[pl.BlockSpec((MOE_TM, D_MODEL), lambda i, f, te, act: (i, 0)),
                  pl.BlockSpec((1, D_MODEL, MOE_TF), lambda i, f, te, act: (te[i], 0, f)),
                  pl.BlockSpec((1, D_MODEL, MOE_TF), lambda i, f, te, act: (te[i], 0, f)),
                  pl.BlockSpec((1, MOE_TF, D_MODEL), lambda i, f, te, act: (te[i], f, 0))],
        out_specs=pl.BlockSpec((MOE_TM, D_MODEL), lambda i, f, te, act: (i, 0)),
    )
    return pl.pallas_call(
        _moe_kernel,
        grid_spec=grid_spec,
        out_shape=jax.ShapeDtypeStruct((n_tiles * MOE_TM, D_MODEL), f32),
        compiler_params=_cparams(("arbitrary", "arbitrary")),
        name="moe_gmm",
    )(tile_expert, tile_active, xs, w1, w3, w2)


def _combine_kernel(ra_ref, rb_ref, ga_ref, gb_ref, x_ref, y_hbm, g_ref, b_ref, o_ref, ya, yb, sem_a, sem_b):
    def issue(r, carry):
        _row_copy(y_hbm, ra_ref[0, 0, r], ya, r, sem_a).start()
        _row_copy(y_hbm, rb_ref[0, 0, r], yb, r, sem_b).start()
        return carry

    lax.fori_loop(0, COMBINE_TM, issue, 0)

    def drain(r, carry):
        _row_copy(y_hbm, 0, ya, r, sem_a).wait()
        _row_copy(y_hbm, 0, yb, r, sem_b).wait()
        return carry

    lax.fori_loop(0, COMBINE_TM, drain, 0)
    f = ga_ref[...] * ya[...] + gb_ref[...] * yb[...]
    o_ref[...] = _layer_norm_rows(DEEPNORM_ALPHA * x_ref[...] + f, g_ref[...], b_ref[...])


def _combine(ra, rb, ga, gb, x, ys, ln_g, ln_b):
    n = x.shape[0]
    smem = pl.BlockSpec((1, 1, COMBINE_TM), lambda i: (i, 0, 0), memory_space=pltpu.SMEM)
    col = pl.BlockSpec((COMBINE_TM, 1), lambda i: (i, 0))
    row = pl.BlockSpec((COMBINE_TM, D_MODEL), lambda i: (i, 0))
    return pl.pallas_call(
        _combine_kernel,
        grid=(n // COMBINE_TM,),
        in_specs=[smem, smem, col, col, row, pl.BlockSpec(memory_space=pl.ANY),
                  _const_spec((1, D_MODEL)), _const_spec((1, D_MODEL))],
        out_specs=row,
        out_shape=jax.ShapeDtypeStruct((n, D_MODEL), f32),
        scratch_shapes=[pltpu.VMEM((COMBINE_TM, D_MODEL), f32), pltpu.VMEM((COMBINE_TM, D_MODEL), f32),
                        pltpu.SemaphoreType.DMA(()), pltpu.SemaphoreType.DMA(())],
        compiler_params=_cparams(("arbitrary",)),
        name="moe_combine",
    )(ra, rb, ga, gb, x, ys, ln_g, ln_b)


def _moe_plan(sel, gates):
    n = sel.shape[0]
    rows_total = TOP_K * n + N_EXPERTS * MOE_TM
    n_tiles = rows_total // MOE_TM
    seli = sel.astype(jnp.int32)
    cnt = jnp.sum(seli, axis=0)
    pos = jnp.cumsum(seli, axis=0) - seli
    padded = ((cnt + MOE_TM - 1) // MOE_TM) * MOE_TM
    bounds = jnp.cumsum(padded)
    row = (bounds - padded)[None, :] + pos
    tok = jnp.broadcast_to(jnp.arange(n, dtype=jnp.int32)[:, None], row.shape)
    tok_of_row = jnp.zeros((rows_total,), jnp.int32).at[jnp.where(sel, row, rows_total)].set(tok, mode='drop')
    tile_start = jnp.arange(n_tiles, dtype=jnp.int32) * MOE_TM
    last_used = jnp.max(jnp.where(cnt > 0, jnp.arange(N_EXPERTS, dtype=jnp.int32), 0))
    tile_expert = jnp.minimum(jnp.searchsorted(bounds, tile_start, side='right').astype(jnp.int32), last_used)
    tile_active = (tile_start < bounds[-1]).astype(jnp.int32)
    eid = jnp.arange(N_EXPERTS, dtype=jnp.int32)[None, :]
    ea = jnp.min(jnp.where(sel, eid, N_EXPERTS), axis=1, keepdims=True)
    eb = jnp.max(jnp.where(sel, eid, -1), axis=1, keepdims=True)
    take = lambda a, e: jnp.take_along_axis(a, e, axis=1)
    return (tok_of_row, tile_expert, tile_active,
            take(row, ea)[:, 0].astype(jnp.int32), take(row, eb)[:, 0].astype(jnp.int32),
            take(gates, ea), take(gates, eb))


def _to_scan_order(x2d):
    n, d = x2d.shape
    return x2d.reshape(n // S5_TB, SUBLANE, S5_T, d).transpose(0, 2, 1, 3).reshape(n, d)


def _from_scan_order(x2d):
    n, d = x2d.shape
    return x2d.reshape(n // S5_TB, S5_T, SUBLANE, d).transpose(0, 2, 1, 3).reshape(n, d)


def _to_strided(x3d, dilation):
    b, s, d = x3d.shape
    return x3d.reshape(b, s // dilation, dilation, d).transpose(0, 2, 1, 3).reshape(b * s, d)


def _from_strided(x2d, batch, dilation):
    n, d = x2d.shape
    s = n // batch
    return x2d.reshape(batch, dilation, s // dilation, d).transpose(0, 2, 1, 3).reshape(n, d)


def kernel(x, s5_w_in, s5_a_re, s5_a_im, s5_log_dt, s5_b_re, s5_b_im, s5_c_re, s5_c_im, s5_d, s5_w_glu,
           attn_w_qkv, attn_w_o, rel_bias, ffn_w1, ffn_w3, ffn_w2, moe_w_router, moe_w1, moe_w3, moe_w2,
           ln_g, ln_b):
    batch, seq, d = x.shape
    n = batch * seq
    assert d == D_MODEL
    assert seq % (DILATED_PAIRS[-1][1] * ATTN_BLOCK * ATT_QB) == 0 and seq % S5_TB == 0
    vec = lambda v: v.reshape(1, D_MODEL).astype(f32)

    bre, bim, cre, cimn, tab = _s5_tables(s5_a_re[0], s5_a_im[0], s5_log_dt[0], s5_b_re[0], s5_b_im[0],
                                          s5_c_re[0], s5_c_im[0])
    xp = _to_scan_order(x.reshape(n, d))
    x1 = _s5_layer(xp, s5_w_in[0].astype(bf16), bre, bim, cre, cimn, tab, vec(s5_d[0]),
                   s5_w_glu[0].astype(bf16), vec(ln_g[0, 0]), vec(ln_b[0, 0]), batch)
    x2 = _ffn_layer(x1, ffn_w1[0].astype(bf16), ffn_w3[0].astype(bf16), ffn_w2[0].astype(bf16),
                    vec(ln_g[0, 1]), vec(ln_b[0, 1]))
    x2 = _from_scan_order(x2)

    xb = x2.astype(bf16).reshape(batch, seq, d)
    xg = jnp.stack([_to_strided(xb, dil) for _, dil in DILATED_PAIRS], axis=0)
    wq = attn_w_qkv[0].reshape(d, N_DIL_GROUPS, 3 * d).transpose(1, 0, 2).astype(bf16)
    qkv = _qkv_proj(xg, wq)
    blocks_per_seq = [seq // (dil * ATTN_BLOCK) for _, dil in DILATED_PAIRS]
    o, lse = _attention(qkv, _attn_bias(rel_bias), blocks_per_seq)
    o = jnp.stack([_from_strided(o[g], batch, dil) for g, (_, dil) in enumerate(DILATED_PAIRS)], axis=0)
    lse = jnp.stack([_from_strided(lse[g], batch, dil) for g, (_, dil) in enumerate(DILATED_PAIRS)], axis=0)
    expand = (jnp.arange(LANE)[:, None] == (jnp.arange(D_MODEL)[None, :] // HEAD_DIM)).astype(bf16)
    wr = jnp.pad(moe_w_router[0].astype(f32), ((0, 0), (0, LANE - N_EXPERTS)))
    wr_hi, wr_lo = _split_bf16(wr)
    x3, gates, sel = _merge_layer(o, lse, x2, expand, attn_w_o[0].astype(bf16), vec(ln_g[1, 0]),
                                  vec(ln_b[1, 0]), wr_hi, wr_lo)

    tok_of_row, tile_expert, tile_active, ra, rb, ga, gb = _moe_plan(sel[:, :N_EXPERTS] > 0.5,
                                                                     gates[:, :N_EXPERTS])
    xs = _gather_rows(tok_of_row.reshape(-1, 1, GATHER_TM), x3)
    ys = _moe_gmm(tile_expert, tile_active, xs, moe_w1[0].astype(bf16), moe_w3[0].astype(bf16),
                  moe_w2[0].astype(bf16))
    out = _combine(ra.reshape(-1, 1, COMBINE_TM), rb.reshape(-1, 1, COMBINE_TM), ga, gb, x3, ys,
                   vec(ln_g[1, 1]), vec(ln_b[1, 1]))
    return out.reshape(batch, seq, d)
```

```python
import functools
import math

import numpy as np
import jax
import jax.numpy as jnp
from jax import lax
from jax.experimental import pallas as pl
from jax.experimental.pallas import tpu as pltpu

D_MODEL = 1024
S5_GROUP = 16
S5_GROUPS = D_MODEL // S5_GROUP
S5_STATE = 64
S5_WIDTH = S5_GROUPS * S5_STATE
HEAD_DIM = 64
N_HEADS = D_MODEL // HEAD_DIM
DILATED_PAIRS = ((128, 1), (512, 4), (2048, 16))
N_DIL_GROUPS = len(DILATED_PAIRS)
ATTN_BLOCK = 128
N_BUCKETS = 32
MAX_DISTANCE = 2048
N_EXPERTS = 8
TOP_K = 2
DEPTH = 2
DEEPNORM_ALPHA = (2 * DEPTH) ** 0.25
LN_EPS = 1e-5
NEG_INF = -1e30

LANE = 128
SUBLANE = 8
MXU_DIM = 256
VMEM_LIMIT = 56 * 1024 * 1024

S5_T = 32
S5_TB = SUBLANE * S5_T
S5_LC = 512
S5_KT = MXU_DIM
S5_NT = D_MODEL // S5_KT
S5_ST = S5_WIDTH // S5_NT
FFN_TM = 512
QKV_TM = 1024
ATT_QB = 4
MRG_TM = 512
MOE_TM = 512
MOE_TF = 1792
GATHER_TM = 512
COMBINE_TM = 256

f32 = jnp.float32
bf16 = jnp.bfloat16


def _cparams(sem):
    return pltpu.CompilerParams(dimension_semantics=sem, vmem_limit_bytes=VMEM_LIMIT)


def _const_spec(shape):
    nd = len(shape)
    return pl.BlockSpec(shape, lambda *_: (0,) * nd, pipeline_mode=pl.Buffered(1))


def _layer_norm_rows(v, g, b):
    mu = jnp.mean(v, axis=-1, keepdims=True)
    c = v - mu
    var = jnp.mean(c * c, axis=-1, keepdims=True)
    return c * lax.rsqrt(var + LN_EPS) * g + b


def _gelu_tanh(x):
    return 0.5 * x * (1.0 + jnp.tanh(math.sqrt(2.0 / math.pi) * (x + 0.044715 * (x * x * x))))


def _dot(a, b):
    return jnp.dot(a, b, preferred_element_type=f32)


_TAB_A = 0
_TAB_P1 = 2
_TAB_P2 = 4
_TAB_P4 = 6
_TAB_PC_RE = 8
_TAB_PC_IM = 16
_TAB_ROWS = 24


def _s5_kernel(x_ref, win_ref, bre_ref, bim_ref, cre_ref, cimn_ref, tab_ref, d_ref, wglu_ref,
               g_ref, b_ref, o_ref, u_ref, hre_ref, him_ref, car_re, car_im):
    T = S5_T
    x = x_ref[...]
    u_ref[...] = _dot(x.astype(bf16), win_ref[...])

    for kt in range(S5_NT):
        ub = u_ref[:, kt * S5_KT:(kt + 1) * S5_KT].astype(bf16)
        hre_ref[:, kt * S5_ST:(kt + 1) * S5_ST] = _dot(ub, bre_ref[kt])
        him_ref[:, kt * S5_ST:(kt + 1) * S5_ST] = _dot(ub, bim_ref[kt])

    @pl.when(pl.program_id(1) == 0)
    def _():
        car_re[...] = jnp.zeros_like(car_re)
        car_im[...] = jnp.zeros_like(car_im)

    row = lax.broadcasted_iota(jnp.int32, (SUBLANE, S5_LC), 0)

    def bcast(r, sl):
        return jnp.broadcast_to(tab_ref[r:r + 1, sl], (SUBLANE, S5_LC))

    def cmul(ar, ai, xr, xi):
        return ar * xr - ai * xi, ar * xi + ai * xr

    for lc in range(S5_WIDTH // S5_LC):
        sl = slice(lc * S5_LC, (lc + 1) * S5_LC)
        ar, ai = bcast(_TAB_A, sl), bcast(_TAB_A + 1, sl)

        def scan_body(t, carry):
            hr, hi = carry
            r0 = pl.multiple_of(t * SUBLANE, SUBLANE)
            pr, pi = cmul(ar, ai, hr, hi)
            nr = pr + hre_ref[pl.ds(r0, SUBLANE), sl]
            ni = pi + him_ref[pl.ds(r0, SUBLANE), sl]
            hre_ref[pl.ds(r0, SUBLANE), sl] = nr
            him_ref[pl.ds(r0, SUBLANE), sl] = ni
            return nr, ni

        zero = jnp.zeros((SUBLANE, S5_LC), f32)
        er, ei = lax.fori_loop(0, T, scan_body, (zero, zero), unroll=2)

        xr, xi = er, ei
        for s, tr in ((1, _TAB_P1), (2, _TAB_P2), (4, _TAB_P4)):
            sr = jnp.where(row >= s, pltpu.roll(xr, s, 0), 0.0)
            si = jnp.where(row >= s, pltpu.roll(xi, s, 0), 0.0)
            mr, mi = cmul(bcast(tr, sl), bcast(tr + 1, sl), sr, si)
            xr, xi = xr + mr, xi + mi
        shr = jnp.where(row >= 1, pltpu.roll(xr, 1, 0), 0.0)
        shi = jnp.where(row >= 1, pltpu.roll(xi, 1, 0), 0.0)
        cr, ci = car_re[:, sl], car_im[:, sl]
        pcr = tab_ref[_TAB_PC_RE:_TAB_PC_RE + SUBLANE, sl]
        pci = tab_ref[_TAB_PC_IM:_TAB_PC_IM + SUBLANE, sl]
        hr0, hi0 = cmul(pcr, pci, cr, ci)
        hr0, hi0 = hr0 + shr, hi0 + shi
        nr, ni = cmul(bcast(_TAB_P1, sl), bcast(_TAB_P1 + 1, sl), hr0, hi0)
        nr, ni = nr + er, ni + ei
        car_re[:, sl] = jnp.broadcast_to(nr[SUBLANE - 1:SUBLANE], (SUBLANE, S5_LC))
        car_im[:, sl] = jnp.broadcast_to(ni[SUBLANE - 1:SUBLANE], (SUBLANE, S5_LC))

        def fix_body(t, carry):
            zr, zi = carry
            r0 = pl.multiple_of(t * SUBLANE, SUBLANE)
            zr, zi = cmul(ar, ai, zr, zi)
            hre_ref[pl.ds(r0, SUBLANE), sl] = hre_ref[pl.ds(r0, SUBLANE), sl] + zr
            him_ref[pl.ds(r0, SUBLANE), sl] = him_ref[pl.ds(r0, SUBLANE), sl] + zi
            return zr, zi

        lax.fori_loop(0, T, fix_body, (hr0, hi0), unroll=2)

    ys = []
    for nt in range(S5_NT):
        hr = hre_ref[:, nt * S5_ST:(nt + 1) * S5_ST].astype(bf16)
        hi = him_ref[:, nt * S5_ST:(nt + 1) * S5_ST].astype(bf16)
        csl = slice(nt * S5_KT, (nt + 1) * S5_KT)
        ys.append(_dot(hr, cre_ref[nt]) + _dot(hi, cimn_ref[nt]) + d_ref[:, csl] * u_ref[:, csl])
    y = jnp.concatenate(ys, axis=1)
    vg = _dot(_gelu_tanh(y).astype(bf16), wglu_ref[...])
    h = vg[:, :D_MODEL] * jax.nn.sigmoid(vg[:, D_MODEL:])
    o_ref[...] = _layer_norm_rows(DEEPNORM_ALPHA * x + h, g_ref[...], b_ref[...])


def _s5_tables(a_re, a_im, log_dt, b_re, b_im, c_re, c_im):
    dt = jnp.exp(log_dt.astype(f32))[:, None]
    lam_re = jnp.minimum(a_re.astype(f32), -1e-4)
    lam_im = a_im.astype(f32)
    mag = jnp.exp(lam_re * dt)
    ab_re = mag * jnp.cos(lam_im * dt)
    ab_im = mag * jnp.sin(lam_im * dt)
    den = lam_re * lam_re + lam_im * lam_im
    nr = ab_re - 1.0
    q_re = (nr * lam_re + ab_im * lam_im) / den
    q_im = (ab_im * lam_re - nr * lam_im) / den
    br, bi = b_re.astype(f32), b_im.astype(f32)
    bb_re = q_re[..., None] * br - q_im[..., None] * bi
    bb_im = q_re[..., None] * bi + q_im[..., None] * br

    gpt = S5_KT // S5_GROUP
    eye = jnp.eye(gpt, dtype=f32)

    def b_blocks(bb):
        t = bb.reshape(S5_NT, gpt, S5_STATE, S5_GROUP)
        blk = jnp.einsum('ngpc,gh->ngchp', t, eye)
        return blk.reshape(S5_NT, S5_KT, S5_ST).astype(bf16)

    def c_blocks(cc):
        t = cc.reshape(S5_NT, gpt, S5_GROUP, S5_STATE)
        blk = jnp.einsum('ngcp,gh->ngphc', t, eye)
        return blk.reshape(S5_NT, S5_ST, S5_KT).astype(bf16)

    def csq(z):
        return z[0] * z[0] - z[1] * z[1], 2.0 * z[0] * z[1]

    def cmul(x, y):
        return x[0] * y[0] - x[1] * y[1], x[0] * y[1] + x[1] * y[0]

    a1 = (ab_re.reshape(-1), ab_im.reshape(-1))
    p = a1
    for _ in range(int(math.log2(S5_T))):
        p = csq(p)
    p1 = p
    p2 = csq(p1)
    p4 = csq(p2)
    pc = [(jnp.ones_like(p1[0]), jnp.zeros_like(p1[0]))]
    for _ in range(SUBLANE - 1):
        pc.append(cmul(pc[-1], p1))
    rows = [a1[0], a1[1], p1[0], p1[1], p2[0], p2[1], p4[0], p4[1]]
    rows += [z[0] for z in pc] + [z[1] for z in pc]
    tab = jnp.stack(rows, axis=0)
    return (b_blocks(bb_re), b_blocks(bb_im), c_blocks(c_re.astype(f32)), c_blocks(-c_im.astype(f32)), tab)


def _s5_layer(xp, w_in, bre, bim, cre, cimn, tab, d_skip, w_glu, ln_g, ln_b, batch):
    n = xp.shape[0]
    steps = n // batch // S5_TB
    row_spec = pl.BlockSpec((S5_TB, D_MODEL), lambda b, t: (b * steps + t, 0))
    return pl.pallas_call(
        _s5_kernel,
        grid=(batch, steps),
        in_specs=[row_spec,
                  _const_spec((D_MODEL, D_MODEL)),
                  _const_spec((S5_NT, S5_KT, S5_ST)), _const_spec((S5_NT, S5_KT, S5_ST)),
                  _const_spec((S5_NT, S5_ST, S5_KT)), _const_spec((S5_NT, S5_ST, S5_KT)),
                  _const_spec((_TAB_ROWS, S5_WIDTH)),
                  _const_spec((1, D_MODEL)),
                  _const_spec((D_MODEL, 2 * D_MODEL)),
                  _const_spec((1, D_MODEL)), _const_spec((1, D_MODEL))],
        out_specs=row_spec,
        out_shape=jax.ShapeDtypeStruct((n, D_MODEL), f32),
        scratch_shapes=[pltpu.VMEM((S5_TB, D_MODEL), f32),
                        pltpu.VMEM((S5_TB, S5_WIDTH), f32), pltpu.VMEM((S5_TB, S5_WIDTH), f32),
                        pltpu.VMEM((SUBLANE, S5_WIDTH), f32), pltpu.VMEM((SUBLANE, S5_WIDTH), f32)],
        compiler_params=_cparams(("arbitrary", "arbitrary")),
        name="s5_mixer",
    )(xp, w_in, bre, bim, cre, cimn, tab, d_skip, w_glu, ln_g, ln_b)


def _ffn_kernel(x_ref, w1_ref, w3_ref, w2_ref, g_ref, b_ref, o_ref):
    x = x_ref[...]
    xb = x.astype(bf16)
    h = jax.nn.silu(_dot(xb, w1_ref[...])) * _dot(xb, w3_ref[...])
    f = _dot(h.astype(bf16), w2_ref[...])
    o_ref[...] = _layer_norm_rows(DEEPNORM_ALPHA * x + f, g_ref[...], b_ref[...])


def _ffn_layer(x, w1, w3, w2, ln_g, ln_b):
    n = x.shape[0]
    d_ff = w1.shape[1]
    row_spec = pl.BlockSpec((FFN_TM, D_MODEL), lambda i: (i, 0))
    return pl.pallas_call(
        _ffn_kernel,
        grid=(n // FFN_TM,),
        in_specs=[row_spec, _const_spec((D_MODEL, d_ff)), _const_spec((D_MODEL, d_ff)),
                  _const_spec((d_ff, D_MODEL)), _const_spec((1, D_MODEL)), _const_spec((1, D_MODEL))],
        out_specs=row_spec,
        out_shape=jax.ShapeDtypeStruct((n, D_MODEL), f32),
        compiler_params=_cparams(("arbitrary",)),
        name="dense_ffn",
    )(x, w1, w3, w2, ln_g, ln_b)


def _qkv_kernel(x_ref, w_ref, o_ref):
    scale = jnp.where(pl.program_id(2) == 0, 1.0 / math.sqrt(HEAD_DIM), 1.0).astype(f32)
    o_ref[0] = (_dot(x_ref[0], w_ref[0]) * scale).astype(o_ref.dtype)


def _qkv_proj(xg, wg):
    ng, n, _ = xg.shape
    return pl.pallas_call(
        _qkv_kernel,
        grid=(ng, n // QKV_TM, 3),
        in_specs=[pl.BlockSpec((1, QKV_TM, D_MODEL), lambda g, i, j: (g, i, 0)),
                  pl.BlockSpec((1, D_MODEL, D_MODEL), lambda g, i, j: (g, 0, j))],
        out_specs=pl.BlockSpec((1, QKV_TM, D_MODEL), lambda g, i, j: (g, i, j)),
        out_shape=jax.ShapeDtypeStruct((ng, n, 3 * D_MODEL), bf16),
        compiler_params=_cparams(("arbitrary", "arbitrary", "arbitrary")),
        name="qkv_proj",
    )(xg, wg)


def _t5_bucket(dist):
    max_exact = N_BUCKETS // 2
    d = np.maximum(dist, 0)
    large = max_exact + (np.log(np.maximum(d, 1) / max_exact) / math.log(MAX_DISTANCE / max_exact)
                         * (N_BUCKETS - max_exact)).astype(np.int64)
    large = np.minimum(large, N_BUCKETS - 1)
    return np.where(d < max_exact, d, large).astype(np.int32)


def _attn_bias(rel_bias):
    L = ATTN_BLOCK
    qi = np.arange(L)[:, None]
    ki = np.arange(2 * L)[None, :]
    off = qi + L - ki
    out = []
    for window, dilation in DILATED_PAIRS:
        band = (off >= 0) & (off <= window // dilation)
        bucket = _t5_bucket(np.clip(off, 0, None) * dilation)
        bias = jnp.transpose(rel_bias[bucket], (2, 0, 1)).astype(f32)
        out.append(jnp.where(band[None], bias, NEG_INF))
    return jnp.stack(out, axis=0)


def _attn_kernel(nb_ref, q_ref, kp_ref, kc_ref, vp_ref, vc_ref, bias_ref, o_ref, lse_ref):
    L = ATTN_BLOCK
    g, i = pl.program_id(0), pl.program_id(1)
    lane = lax.broadcasted_iota(jnp.int32, (L, LANE), 1)
    low = lane < HEAD_DIM
    is_first = lax.rem(i * ATT_QB, nb_ref[g]) == 0
    col = lax.broadcasted_iota(jnp.int32, (L, 2 * L), 1)
    pen = jnp.where((col < L) & is_first, NEG_INF, 0.0).astype(f32)
    for s in range(ATT_QB):
        rs = slice(s * L, (s + 1) * L)
        q = q_ref[0, rs, :]
        if s == 0:
            k = jnp.concatenate([kp_ref[0], kc_ref[0, rs, :]], axis=0)
            v = jnp.concatenate([vp_ref[0], vc_ref[0, rs, :]], axis=0)
        else:
            ks = slice((s - 1) * L, (s + 1) * L)
            k = kc_ref[0, ks, :]
            v = vc_ref[0, ks, :]
        lse_acc = jnp.zeros((L, LANE), f32)
        for hp in range(N_HEADS // 2):
            cs = slice(hp * LANE, (hp + 1) * LANE)
            qp, kpair, vpair = q[:, cs], k[:, cs], v[:, cs]
            outs = []
            for half in range(2):
                h = 2 * hp + half
                qm = jnp.where(low if half == 0 else jnp.logical_not(low), qp, jnp.zeros_like(qp))
                logits = lax.dot_general(qm, kpair, (((1,), (1,)), ((), ())),
                                         preferred_element_type=f32)
                logits = logits + bias_ref[0, h]
                if s == 0:
                    logits = logits + pen
                m = jnp.max(logits, axis=-1, keepdims=True)
                p = jnp.exp(logits - m)
                ssum = jnp.sum(p, axis=-1, keepdims=True)
                outs.append(_dot(p.astype(bf16), vpair) / ssum)
                lse_acc = jnp.where(lane == h, m + jnp.log(ssum), lse_acc)
            o_ref[0, rs, cs] = jnp.where(low, outs[0], outs[1]).astype(o_ref.dtype)
        lse_ref[0, rs, :] = lse_acc


def _attention(qkv, bias, blocks_per_seq):
    ng, n, _ = qkv.shape
    L = ATTN_BLOCK
    tq = ATT_QB * L

    def cur(col):
        return pl.BlockSpec((1, tq, D_MODEL), lambda g, i, nb: (g, i, col))

    def prev(col):
        return pl.BlockSpec((1, L, D_MODEL), lambda g, i, nb: (g, jnp.maximum(i * ATT_QB - 1, 0), col))

    grid_spec = pltpu.PrefetchScalarGridSpec(
        num_scalar_prefetch=1,
        grid=(ng, n // tq),
        in_specs=[cur(0), prev(1), cur(1), prev(2), cur(2),
                  pl.BlockSpec((1, N_HEADS, L, 2 * L), lambda g, i, nb: (g, 0, 0, 0))],
        out_specs=[pl.BlockSpec((1, tq, D_MODEL), lambda g, i, nb: (g, i, 0)),
                   pl.BlockSpec((1, tq, LANE), lambda g, i, nb: (g, i, 0))],
    )
    return pl.pallas_call(
        _attn_kernel,
        grid_spec=grid_spec,
        out_shape=[jax.ShapeDtypeStruct((ng, n, D_MODEL), bf16),
                   jax.ShapeDtypeStruct((ng, n, LANE), f32)],
        compiler_params=_cparams(("arbitrary", "arbitrary")),
        name="dilated_attn",
    )(jnp.asarray(blocks_per_seq, jnp.int32), qkv, qkv, qkv, qkv, qkv, bias)


def _split_bf16(v):
    hi = v.astype(bf16)
    return hi, (v - hi.astype(f32)).astype(bf16)


def _merge_kernel(o_ref, lse_ref, x_ref, e_ref, wo_ref, g_ref, b_ref, wrh_ref, wrl_ref,
                  x3_ref, gate_ref, sel_ref):
    ls = [lse_ref[g] for g in range(N_DIL_GROUPS)]
    mx = functools.reduce(jnp.maximum, ls)
    es = [jnp.exp(l - mx) for l in ls]
    den = functools.reduce(jnp.add, es)
    merged = None
    for g in range(N_DIL_GROUPS):
        wh, wl = _split_bf16(es[g] / den)
        wfull = _dot(wh, e_ref[...]) + _dot(wl, e_ref[...])
        term = wfull * o_ref[g].astype(f32)
        merged = term if merged is None else merged + term
    att = _dot(merged.astype(bf16), wo_ref[...])
    x3 = _layer_norm_rows(DEEPNORM_ALPHA * x_ref[...] + att, g_ref[...], b_ref[...])
    x3_ref[...] = x3

    xh, xl = _split_bf16(x3)
    logits = _dot(xh, wrh_ref[...]) + _dot(xl, wrh_ref[...]) + _dot(xh, wrl_ref[...])
    lane = lax.broadcasted_iota(jnp.int32, logits.shape, 1)
    neg = -jnp.inf
    logits = jnp.where(lane < N_EXPERTS, logits, neg)
    m1 = jnp.max(logits, axis=-1, keepdims=True)
    i1 = jnp.min(jnp.where(logits == m1, lane, LANE), axis=-1, keepdims=True)
    pick1 = lane == i1
    rest = jnp.where(pick1, neg, logits)
    m2 = jnp.max(rest, axis=-1, keepdims=True)
    i2 = jnp.min(jnp.where(rest == m2, lane, LANE), axis=-1, keepdims=True)
    pick2 = lane == i2
    e2 = jnp.exp(m2 - m1)
    gate_ref[...] = jnp.where(pick1, 1.0 / (1.0 + e2), jnp.where(pick2, e2 / (1.0 + e2), 0.0))
    sel_ref[...] = jnp.where(pick1 | pick2, 1.0, 0.0)


def _merge_layer(o, lse, x, expand, w_o, ln_g, ln_b, wr_hi, wr_lo):
    n = x.shape[0]
    row = lambda w: pl.BlockSpec((MRG_TM, w), lambda i: (i, 0))
    grp = lambda w: pl.BlockSpec((N_DIL_GROUPS, MRG_TM, w), lambda i: (0, i, 0))
    return pl.pallas_call(
        _merge_kernel,
        grid=(n // MRG_TM,),
        in_specs=[grp(D_MODEL), grp(LANE), row(D_MODEL), _const_spec((LANE, D_MODEL)),
                  _const_spec((D_MODEL, D_MODEL)), _const_spec((1, D_MODEL)), _const_spec((1, D_MODEL)),
                  _const_spec((D_MODEL, LANE)), _const_spec((D_MODEL, LANE))],
        out_specs=[row(D_MODEL), row(LANE), row(LANE)],
        out_shape=[jax.ShapeDtypeStruct((n, D_MODEL), f32), jax.ShapeDtypeStruct((n, LANE), f32),
                   jax.ShapeDtypeStruct((n, LANE), f32)],
        compiler_params=_cparams(("arbitrary",)),
        name="attn_merge",
    )(o, lse, x, expand, w_o, ln_g, ln_b, wr_hi, wr_lo)


def _row_copy(src_hbm, src_row, dst_ref, dst_row, sem):
    return pltpu.make_async_copy(src_hbm.at[pl.ds(src_row, 1), :], dst_ref.at[pl.ds(dst_row, 1), :], sem)


def _dispatch_kernel(pad_start_ref, pad_cnt_ref, ra_ref, rb_ref, x_ref, xs_hbm, zbuf, sem, zsem):
    def fill_copy(e, k):
        return _row_copy(zbuf, 0, xs_hbm, pad_start_ref[e] + k, zsem)

    def tail_copy(k):
        start = pl.multiple_of(pad_start_ref[N_EXPERTS] + k * MOE_TM, MOE_TM)
        return pltpu.make_async_copy(zbuf, xs_hbm.at[pl.ds(start, MOE_TM), :], zsem)

    @pl.when(pl.program_id(0) == 0)
    def _():
        zbuf[...] = jnp.zeros_like(zbuf)
        for e in range(N_EXPERTS):
            lax.fori_loop(0, pad_cnt_ref[e], lambda k, c: (fill_copy(e, k).start(), c)[1], 0)
        lax.fori_loop(0, pad_cnt_ref[N_EXPERTS], lambda k, c: (tail_copy(k).start(), c)[1], 0)

    def issue(r, carry):
        _row_copy(x_ref, r, xs_hbm, ra_ref[0, 0, r], sem).start()
        _row_copy(x_ref, r, xs_hbm, rb_ref[0, 0, r], sem).start()
        return carry

    lax.fori_loop(0, GATHER_TM, issue, 0)

    def drain(r, carry):
        _row_copy(x_ref, r, xs_hbm, 0, sem).wait()
        _row_copy(x_ref, r, xs_hbm, 0, sem).wait()
        return carry

    lax.fori_loop(0, GATHER_TM, drain, 0)

    @pl.when(pl.program_id(0) == 0)
    def _():
        for e in range(N_EXPERTS):
            lax.fori_loop(0, pad_cnt_ref[e], lambda k, c: (fill_copy(e, k).wait(), c)[1], 0)
        lax.fori_loop(0, pad_cnt_ref[N_EXPERTS], lambda k, c: (tail_copy(k).wait(), c)[1], 0)


def _dispatch_rows(pad_start, pad_cnt, ra, rb, x, rows_total):
    n = x.shape[0]
    smem = pl.BlockSpec((1, 1, GATHER_TM), lambda i, ps, pc: (i, 0, 0), memory_space=pltpu.SMEM)
    grid_spec = pltpu.PrefetchScalarGridSpec(
        num_scalar_prefetch=2,
        grid=(n // GATHER_TM,),
        in_specs=[smem, smem, pl.BlockSpec((GATHER_TM, D_MODEL), lambda i, ps, pc: (i, 0))],
        out_specs=pl.BlockSpec(memory_space=pl.ANY),
        scratch_shapes=[pltpu.VMEM((MOE_TM, D_MODEL), x.dtype), pltpu.SemaphoreType.DMA(()),
                        pltpu.SemaphoreType.DMA(())],
    )
    return pl.pallas_call(
        _dispatch_kernel,
        grid_spec=grid_spec,
        out_shape=jax.ShapeDtypeStruct((rows_total, D_MODEL), x.dtype),
        compiler_params=_cparams(("arbitrary",)),
        name="moe_dispatch",
    )(pad_start, pad_cnt, ra, rb, x)


def _moe_kernel(te_ref, act_ref, x_ref, w1_ref, w3_ref, w2_ref, o_ref):
    del te_ref
    i, f = pl.program_id(0), pl.program_id(1)

    @pl.when(act_ref[i] == 1)
    def _():
        xb = x_ref[...].astype(bf16)
        h = jax.nn.silu(_dot(xb, w1_ref[0])) * _dot(xb, w3_ref[0])
        part = _dot(h.astype(bf16), w2_ref[0])

        @pl.when(f == 0)
        def _():
            o_ref[...] = part

        @pl.when(f > 0)
        def _():
            o_ref[...] = o_ref[...] + part

    @pl.when((act_ref[i] == 0) & (f == 0))
    def _():
        o_ref[...] = jnp.zeros_like(o_ref)


def _moe_gmm(tile_expert, tile_active, xs, w1, w3, w2):
    n_tiles = tile_expert.shape[0]
    d_ff = w1.shape[2]
    grid_spec = pltpu.PrefetchScalarGridSpec(
        num_scalar_prefetch=2,
        grid=(n_tiles, d_ff // MOE_TF),
        in_specs=[pl.BlockSpec((MOE_TM, D_MODEL), lambda i, f, te, act: (i * act[i], 0)),
                  pl.BlockSpec((1, D_MODEL, MOE_TF), lambda i, f, te, act: (te[i], 0, f)),
                  pl.BlockSpec((1, D_MODEL, MOE_TF), lambda i, f, te, act: (te[i], 0, f)),
                  pl.BlockSpec((1, MOE_TF, D_MODEL), lambda i, f, te, act: (te[i], f, 0))],
        out_specs=pl.BlockSpec((MOE_TM, D_MODEL), lambda i, f, te, act: (i, 0)),
    )
    return pl.pallas_call(
        _moe_kernel,
        grid_spec=grid_spec,
        out_shape=jax.ShapeDtypeStruct((n_tiles * MOE_TM, D_MODEL), f32),
        compiler_params=_cparams(("arbitrary", "arbitrary")),
        name="moe_gmm",
    )(tile_expert, tile_active, xs, w1, w3, w2)


def _combine_kernel(ra_ref, rb_ref, ra_next, rb_next, ga_ref, gb_ref, x_ref, y_hbm, g_ref, b_ref, o_ref,
                    ybuf, sem):
    i = pl.program_id(0)
    slot = lax.rem(i, 2)

    def gather(ia_ref, ib_ref, s):
        def issue(r, carry):
            _row_copy(y_hbm, ia_ref[0, 0, r], ybuf.at[s, 0], r, sem.at[s, 0]).start()
            _row_copy(y_hbm, ib_ref[0, 0, r], ybuf.at[s, 1], r, sem.at[s, 1]).start()
            return carry

        lax.fori_loop(0, COMBINE_TM, issue, 0)

    @pl.when(i == 0)
    def _():
        gather(ra_ref, rb_ref, 0)

    @pl.when(i + 1 < pl.num_programs(0))
    def _():
        gather(ra_next, rb_next, 1 - slot)

    def drain(r, carry):
        _row_copy(y_hbm, 0, ybuf.at[slot, 0], r, sem.at[slot, 0]).wait()
        _row_copy(y_hbm, 0, ybuf.at[slot, 1], r, sem.at[slot, 1]).wait()
        return carry

    lax.fori_loop(0, COMBINE_TM, drain, 0)
    f = ga_ref[...] * ybuf[slot, 0] + gb_ref[...] * ybuf[slot, 1]
    o_ref[...] = _layer_norm_rows(DEEPNORM_ALPHA * x_ref[...] + f, g_ref[...], b_ref[...])


def _combine(ra, rb, ga, gb, x, ys, ln_g, ln_b):
    n = x.shape[0]
    last = n // COMBINE_TM - 1
    smem = pl.BlockSpec((1, 1, COMBINE_TM), lambda i: (i, 0, 0), memory_space=pltpu.SMEM)
    smem_next = pl.BlockSpec((1, 1, COMBINE_TM), lambda i: (jnp.minimum(i + 1, last), 0, 0),
                             memory_space=pltpu.SMEM)
    col = pl.BlockSpec((COMBINE_TM, 1), lambda i: (i, 0))
    row = pl.BlockSpec((COMBINE_TM, D_MODEL), lambda i: (i, 0))
    return pl.pallas_call(
        _combine_kernel,
        grid=(n // COMBINE_TM,),
        in_specs=[smem, smem, smem_next, smem_next, col, col, row, pl.BlockSpec(memory_space=pl.ANY),
                  _const_spec((1, D_MODEL)), _const_spec((1, D_MODEL))],
        out_specs=row,
        out_shape=jax.ShapeDtypeStruct((n, D_MODEL), f32),
        scratch_shapes=[pltpu.VMEM((2, TOP_K, COMBINE_TM, D_MODEL), f32), pltpu.SemaphoreType.DMA((2, TOP_K))],
        compiler_params=_cparams(("arbitrary",)),
        name="moe_combine",
    )(ra, rb, ra, rb, ga, gb, x, ys, ln_g, ln_b)


def _moe_plan(sel, gates):
    n = sel.shape[0]
    rows_total = TOP_K * n + N_EXPERTS * MOE_TM
    n_tiles = rows_total // MOE_TM
    seli = sel.astype(jnp.int32)
    cnt = jnp.sum(seli, axis=0)
    pos = jnp.cumsum(seli, axis=0) - seli
    padded = ((cnt + MOE_TM - 1) // MOE_TM) * MOE_TM
    bounds = jnp.cumsum(padded)
    row = (bounds - padded)[None, :] + pos
    pad_start = jnp.concatenate([bounds - padded + cnt, bounds[-1:]]).astype(jnp.int32)
    pad_cnt = jnp.concatenate([padded - cnt, (rows_total - bounds[-1:]) // MOE_TM]).astype(jnp.int32)
    tile_start = jnp.arange(n_tiles, dtype=jnp.int32) * MOE_TM
    last_used = jnp.max(jnp.where(cnt > 0, jnp.arange(N_EXPERTS, dtype=jnp.int32), 0))
    tile_expert = jnp.minimum(jnp.searchsorted(bounds, tile_start, side='right').astype(jnp.int32), last_used)
    tile_active = (tile_start < bounds[-1]).astype(jnp.int32)
    eid = jnp.arange(N_EXPERTS, dtype=jnp.int32)[None, :]
    ea = jnp.min(jnp.where(sel, eid, N_EXPERTS), axis=1, keepdims=True)
    eb = jnp.max(jnp.where(sel, eid, -1), axis=1, keepdims=True)
    take = lambda a, e: jnp.take_along_axis(a, e, axis=1)
    return (pad_start, pad_cnt, tile_expert, tile_active,
            take(row, ea)[:, 0].astype(jnp.int32), take(row, eb)[:, 0].astype(jnp.int32),
            take(gates, ea), take(gates, eb))


def _to_scan_order(x2d):
    n, d = x2d.shape
    return x2d.reshape(n // S5_TB, SUBLANE, S5_T, d).transpose(0, 2, 1, 3).reshape(n, d)


def _from_scan_order(x2d):
    n, d = x2d.shape
    return x2d.reshape(n // S5_TB, S5_T, SUBLANE, d).transpose(0, 2, 1, 3).reshape(n, d)


def _to_strided(x3d, dilation):
    b, s, d = x3d.shape
    return x3d.reshape(b, s // dilation, dilation, d).transpose(0, 2, 1, 3).reshape(b * s, d)


def _from_strided(x2d, batch, dilation):
    n, d = x2d.shape
    s = n // batch
    return x2d.reshape(batch, dilation, s // dilation, d).transpose(0, 2, 1, 3).reshape(n, d)


def kernel(x, s5_w_in, s5_a_re, s5_a_im, s5_log_dt, s5_b_re, s5_b_im, s5_c_re, s5_c_im, s5_d, s5_w_glu,
           attn_w_qkv, attn_w_o, rel_bias, ffn_w1, ffn_w3, ffn_w2, moe_w_router, moe_w1, moe_w3, moe_w2,
           ln_g, ln_b):
    batch, seq, d = x.shape
    n = batch * seq
    assert d == D_MODEL
    assert seq % (DILATED_PAIRS[-1][1] * ATTN_BLOCK * ATT_QB) == 0 and seq % S5_TB == 0
    vec = lambda v: v.reshape(1, D_MODEL).astype(f32)

    bre, bim, cre, cimn, tab = _s5_tables(s5_a_re[0], s5_a_im[0], s5_log_dt[0], s5_b_re[0], s5_b_im[0],
                                          s5_c_re[0], s5_c_im[0])
    xp = _to_scan_order(x.reshape(n, d))
    x1 = _s5_layer(xp, s5_w_in[0].astype(bf16), bre, bim, cre, cimn, tab, vec(s5_d[0]),
                   s5_w_glu[0].astype(bf16), vec(ln_g[0, 0]), vec(ln_b[0, 0]), batch)
    x2 = _ffn_layer(x1, ffn_w1[0].astype(bf16), ffn_w3[0].astype(bf16), ffn_w2[0].astype(bf16),
                    vec(ln_g[0, 1]), vec(ln_b[0, 1]))
    x2 = _from_scan_order(x2)

    xb = x2.astype(bf16).reshape(batch, seq, d)
    xg = jnp.stack([_to_strided(xb, dil) for _, dil in DILATED_PAIRS], axis=0)
    wq = attn_w_qkv[0].reshape(d, N_DIL_GROUPS, 3 * d).transpose(1, 0, 2).astype(bf16)
    qkv = _qkv_proj(xg, wq)
    blocks_per_seq = [seq // (dil * ATTN_BLOCK) for _, dil in DILATED_PAIRS]
    o, lse = _attention(qkv, _attn_bias(rel_bias), blocks_per_seq)
    o = jnp.stack([_from_strided(o[g], batch, dil) for g, (_, dil) in enumerate(DILATED_PAIRS)], axis=0)
    lse = jnp.stack([_from_strided(lse[g], batch, dil) for g, (_, dil) in enumerate(DILATED_PAIRS)], axis=0)
    expand = (jnp.arange(LANE)[:, None] == (jnp.arange(D_MODEL)[None, :] // HEAD_DIM)).astype(bf16)
    wr = jnp.pad(moe_w_router[0].astype(f32), ((0, 0), (0, LANE - N_EXPERTS)))
    wr_hi, wr_lo = _split_bf16(wr)
    x3, gates, sel = _merge_layer(o, lse, x2, expand, attn_w_o[0].astype(bf16), vec(ln_g[1, 0]),
                                  vec(ln_b[1, 0]), wr_hi, wr_lo)

    pad_start, pad_cnt, tile_expert, tile_active, ra, rb, ga, gb = _moe_plan(sel[:, :N_EXPERTS] > 0.5,
                                                                             gates[:, :N_EXPERTS])
    xs = _dispatch_rows(pad_start, pad_cnt, ra.reshape(-1, 1, GATHER_TM), rb.reshape(-1, 1, GATHER_TM), x3,
                        TOP_K * n + N_EXPERTS * MOE_TM)
    ys = _moe_gmm(tile_expert, tile_active, xs, moe_w1[0].astype(bf16), moe_w3[0].astype(bf16),
                  moe_w2[0].astype(bf16))
    out = _combine(ra.reshape(-1, 1, COMBINE_TM), rb.reshape(-1, 1, COMBINE_TM), ga, gb, x3, ys,
                   vec(ln_g[1, 1]), vec(ln_b[1, 1]))
    return out.reshape(batch, seq, d)
```

```python
import functools
import math

import numpy as np
import jax
import jax.numpy as jnp
from jax import lax
from jax.experimental import pallas as pl
from jax.experimental.pallas import tpu as pltpu

D_MODEL = 1024
S5_GROUP = 16
S5_GROUPS = D_MODEL // S5_GROUP
S5_STATE = 64
S5_WIDTH = S5_GROUPS * S5_STATE
HEAD_DIM = 64
N_HEADS = D_MODEL // HEAD_DIM
DILATED_PAIRS = ((128, 1), (512, 4), (2048, 16))
N_DIL_GROUPS = len(DILATED_PAIRS)
ATTN_BLOCK = 128
N_BUCKETS = 32
MAX_DISTANCE = 2048
N_EXPERTS = 8
TOP_K = 2
DEPTH = 2
DEEPNORM_ALPHA = (2 * DEPTH) ** 0.25
LN_EPS = 1e-5
NEG_INF = -1e30

LANE = 128
SUBLANE = 8
MXU_DIM = 256
VMEM_LIMIT = 56 * 1024 * 1024

S5_T = 32
S5_TB = SUBLANE * S5_T
S5_LC = 512
S5_KT = MXU_DIM
S5_NT = D_MODEL // S5_KT
S5_ST = S5_WIDTH // S5_NT
FFN_TM = 512
QKV_TM = 1024
ATT_QB = 4
MRG_TM = 512
MOE_TM = 512
MOE_TF = 1792
GATHER_TM = 512
COMBINE_TM = 256

f32 = jnp.float32
bf16 = jnp.bfloat16


def _cparams(sem):
    return pltpu.CompilerParams(dimension_semantics=sem, vmem_limit_bytes=VMEM_LIMIT)


def _const_spec(shape):
    nd = len(shape)
    return pl.BlockSpec(shape, lambda *_: (0,) * nd, pipeline_mode=pl.Buffered(1))


def _layer_norm_rows(v, g, b):
    mu = jnp.mean(v, axis=-1, keepdims=True)
    c = v - mu
    var = jnp.mean(c * c, axis=-1, keepdims=True)
    return c * lax.rsqrt(var + LN_EPS) * g + b


def _gelu_tanh(x):
    return 0.5 * x * (1.0 + jnp.tanh(math.sqrt(2.0 / math.pi) * (x + 0.044715 * (x * x * x))))


def _dot(a, b):
    return jnp.dot(a, b, preferred_element_type=f32)


_TAB_A = 0
_TAB_P1 = 2
_TAB_P2 = 4
_TAB_P4 = 6
_TAB_PC_RE = 8
_TAB_PC_IM = 16
_TAB_ROWS = 24


def _split3_bf16(v):
    hi = v.astype(bf16)
    r1 = v - hi.astype(f32)
    mid = r1.astype(bf16)
    return hi, mid, (r1 - mid.astype(f32)).astype(bf16)


def _permute_rows_f32(perm, v):
    hi, mid, lo = _split3_bf16(v)
    return _dot(perm, hi) + _dot(perm, mid) + _dot(perm, lo)


def _s5_kernel(x_ref, pf_ref, pb_ref, win_ref, bre_ref, bim_ref, cre_ref, cimn_ref, tab_ref, d_ref, wglu_ref,
               g_ref, b_ref, o_ref, u_ref, hre_ref, him_ref, car_re, car_im):
    T = S5_T
    x = x_ref[...]
    xs = _dot(pf_ref[...], x.astype(bf16)).astype(bf16)
    u_ref[...] = _dot(xs, win_ref[...])

    for kt in range(S5_NT):
        ub = u_ref[:, kt * S5_KT:(kt + 1) * S5_KT].astype(bf16)
        hre_ref[:, kt * S5_ST:(kt + 1) * S5_ST] = _dot(ub, bre_ref[kt])
        him_ref[:, kt * S5_ST:(kt + 1) * S5_ST] = _dot(ub, bim_ref[kt])

    @pl.when(pl.program_id(1) == 0)
    def _():
        car_re[...] = jnp.zeros_like(car_re)
        car_im[...] = jnp.zeros_like(car_im)

    row = lax.broadcasted_iota(jnp.int32, (SUBLANE, S5_LC), 0)

    def bcast(r, sl):
        return jnp.broadcast_to(tab_ref[r:r + 1, sl], (SUBLANE, S5_LC))

    def cmul(ar, ai, xr, xi):
        return ar * xr - ai * xi, ar * xi + ai * xr

    for lc in range(S5_WIDTH // S5_LC):
        sl = slice(lc * S5_LC, (lc + 1) * S5_LC)
        ar, ai = bcast(_TAB_A, sl), bcast(_TAB_A + 1, sl)

        def scan_body(t, carry):
            hr, hi = carry
            r0 = pl.multiple_of(t * SUBLANE, SUBLANE)
            pr, pi = cmul(ar, ai, hr, hi)
            nr = pr + hre_ref[pl.ds(r0, SUBLANE), sl]
            ni = pi + him_ref[pl.ds(r0, SUBLANE), sl]
            hre_ref[pl.ds(r0, SUBLANE), sl] = nr
            him_ref[pl.ds(r0, SUBLANE), sl] = ni
            return nr, ni

        zero = jnp.zeros((SUBLANE, S5_LC), f32)
        er, ei = lax.fori_loop(0, T, scan_body, (zero, zero), unroll=2)

        xr, xi = er, ei
        for s, tr in ((1, _TAB_P1), (2, _TAB_P2), (4, _TAB_P4)):
            sr = jnp.where(row >= s, pltpu.roll(xr, s, 0), 0.0)
            si = jnp.where(row >= s, pltpu.roll(xi, s, 0), 0.0)
            mr, mi = cmul(bcast(tr, sl), bcast(tr + 1, sl), sr, si)
            xr, xi = xr + mr, xi + mi
        shr = jnp.where(row >= 1, pltpu.roll(xr, 1, 0), 0.0)
        shi = jnp.where(row >= 1, pltpu.roll(xi, 1, 0), 0.0)
        cr, ci = car_re[:, sl], car_im[:, sl]
        pcr = tab_ref[_TAB_PC_RE:_TAB_PC_RE + SUBLANE, sl]
        pci = tab_ref[_TAB_PC_IM:_TAB_PC_IM + SUBLANE, sl]
        hr0, hi0 = cmul(pcr, pci, cr, ci)
        hr0, hi0 = hr0 + shr, hi0 + shi
        nr, ni = cmul(bcast(_TAB_P1, sl), bcast(_TAB_P1 + 1, sl), hr0, hi0)
        nr, ni = nr + er, ni + ei
        car_re[:, sl] = jnp.broadcast_to(nr[SUBLANE - 1:SUBLANE], (SUBLANE, S5_LC))
        car_im[:, sl] = jnp.broadcast_to(ni[SUBLANE - 1:SUBLANE], (SUBLANE, S5_LC))

        def fix_body(t, carry):
            zr, zi = carry
            r0 = pl.multiple_of(t * SUBLANE, SUBLANE)
            zr, zi = cmul(ar, ai, zr, zi)
            hre_ref[pl.ds(r0, SUBLANE), sl] = hre_ref[pl.ds(r0, SUBLANE), sl] + zr
            him_ref[pl.ds(r0, SUBLANE), sl] = him_ref[pl.ds(r0, SUBLANE), sl] + zi
            return zr, zi

        lax.fori_loop(0, T, fix_body, (hr0, hi0), unroll=2)

    ys = []
    for nt in range(S5_NT):
        hr = hre_ref[:, nt * S5_ST:(nt + 1) * S5_ST].astype(bf16)
        hi = him_ref[:, nt * S5_ST:(nt + 1) * S5_ST].astype(bf16)
        csl = slice(nt * S5_KT, (nt + 1) * S5_KT)
        ys.append(_dot(hr, cre_ref[nt]) + _dot(hi, cimn_ref[nt]) + d_ref[:, csl] * u_ref[:, csl])
    y = jnp.concatenate(ys, axis=1)
    vg = _dot(_gelu_tanh(y).astype(bf16), wglu_ref[...])
    h = vg[:, :D_MODEL] * jax.nn.sigmoid(vg[:, D_MODEL:])
    h = _permute_rows_f32(pb_ref[...], h)
    o_ref[...] = _layer_norm_rows(DEEPNORM_ALPHA * x + h, g_ref[...], b_ref[...])


def _s5_tables(a_re, a_im, log_dt, b_re, b_im, c_re, c_im):
    dt = jnp.exp(log_dt.astype(f32))[:, None]
    lam_re = jnp.minimum(a_re.astype(f32), -1e-4)
    lam_im = a_im.astype(f32)
    mag = jnp.exp(lam_re * dt)
    ab_re = mag * jnp.cos(lam_im * dt)
    ab_im = mag * jnp.sin(lam_im * dt)
    den = lam_re * lam_re + lam_im * lam_im
    nr = ab_re - 1.0
    q_re = (nr * lam_re + ab_im * lam_im) / den
    q_im = (ab_im * lam_re - nr * lam_im) / den
    br, bi = b_re.astype(f32), b_im.astype(f32)
    bb_re = q_re[..., None] * br - q_im[..., None] * bi
    bb_im = q_re[..., None] * bi + q_im[..., None] * br

    gpt = S5_KT // S5_GROUP
    eye = jnp.eye(gpt, dtype=f32)

    def b_blocks(bb):
        t = bb.reshape(S5_NT, gpt, S5_STATE, S5_GROUP)
        blk = jnp.einsum('ngpc,gh->ngchp', t, eye)
        return blk.reshape(S5_NT, S5_KT, S5_ST).astype(bf16)

    def c_blocks(cc):
        t = cc.reshape(S5_NT, gpt, S5_GROUP, S5_STATE)
        blk = jnp.einsum('ngcp,gh->ngphc', t, eye)
        return blk.reshape(S5_NT, S5_ST, S5_KT).astype(bf16)

    def csq(z):
        return z[0] * z[0] - z[1] * z[1], 2.0 * z[0] * z[1]

    def cmul(x, y):
        return x[0] * y[0] - x[1] * y[1], x[0] * y[1] + x[1] * y[0]

    a1 = (ab_re.reshape(-1), ab_im.reshape(-1))
    p = a1
    for _ in range(int(math.log2(S5_T))):
        p = csq(p)
    p1 = p
    p2 = csq(p1)
    p4 = csq(p2)
    pc = [(jnp.ones_like(p1[0]), jnp.zeros_like(p1[0]))]
    for _ in range(SUBLANE - 1):
        pc.append(cmul(pc[-1], p1))
    rows = [a1[0], a1[1], p1[0], p1[1], p2[0], p2[1], p4[0], p4[1]]
    rows += [z[0] for z in pc] + [z[1] for z in pc]
    tab = jnp.stack(rows, axis=0)
    return (b_blocks(bb_re), b_blocks(bb_im), c_blocks(c_re.astype(f32)), c_blocks(-c_im.astype(f32)), tab)


def _scan_perm():
    p = np.zeros((S5_TB, S5_TB), np.float32)
    for c in range(SUBLANE):
        for t in range(S5_T):
            p[t * SUBLANE + c, c * S5_T + t] = 1.0
    return p


def _stride_perm(tile, dilation):
    p = np.zeros((tile, tile), np.float32)
    per = tile // dilation
    for j in range(per):
        for r in range(dilation):
            p[r * per + j, j * dilation + r] = 1.0
    return p


def _s5_layer(x, w_in, bre, bim, cre, cimn, tab, d_skip, w_glu, ln_g, ln_b, batch):
    n = x.shape[0]
    steps = n // batch // S5_TB
    row_spec = pl.BlockSpec((S5_TB, D_MODEL), lambda b, t: (b * steps + t, 0))
    perm = _scan_perm()
    return pl.pallas_call(
        _s5_kernel,
        grid=(batch, steps),
        in_specs=[row_spec,
                  _const_spec((S5_TB, S5_TB)), _const_spec((S5_TB, S5_TB)),
                  _const_spec((D_MODEL, D_MODEL)),
                  _const_spec((S5_NT, S5_KT, S5_ST)), _const_spec((S5_NT, S5_KT, S5_ST)),
                  _const_spec((S5_NT, S5_ST, S5_KT)), _const_spec((S5_NT, S5_ST, S5_KT)),
                  _const_spec((_TAB_ROWS, S5_WIDTH)),
                  _const_spec((1, D_MODEL)),
                  _const_spec((D_MODEL, 2 * D_MODEL)),
                  _const_spec((1, D_MODEL)), _const_spec((1, D_MODEL))],
        out_specs=row_spec,
        out_shape=jax.ShapeDtypeStruct((n, D_MODEL), f32),
        scratch_shapes=[pltpu.VMEM((S5_TB, D_MODEL), f32),
                        pltpu.VMEM((S5_TB, S5_WIDTH), f32), pltpu.VMEM((S5_TB, S5_WIDTH), f32),
                        pltpu.VMEM((SUBLANE, S5_WIDTH), f32), pltpu.VMEM((SUBLANE, S5_WIDTH), f32)],
        compiler_params=_cparams(("arbitrary", "arbitrary")),
        name="s5_mixer",
    )(x, jnp.asarray(perm, bf16), jnp.asarray(perm.T, bf16), w_in, bre, bim, cre, cimn, tab, d_skip, w_glu,
      ln_g, ln_b)


def _ffn_kernel(x_ref, w1_ref, w3_ref, w2_ref, g_ref, b_ref, *rest):
    perm_refs = rest[:N_DIL_GROUPS - 1]
    o_ref = rest[N_DIL_GROUPS - 1]
    xg_refs = rest[N_DIL_GROUPS:]
    x = x_ref[...]
    xb = x.astype(bf16)
    h = jax.nn.silu(_dot(xb, w1_ref[...])) * _dot(xb, w3_ref[...])
    f = _dot(h.astype(bf16), w2_ref[...])
    y = _layer_norm_rows(DEEPNORM_ALPHA * x + f, g_ref[...], b_ref[...])
    o_ref[...] = y
    yb = y.astype(bf16)
    xg_refs[0][...] = yb
    for perm_ref, xg_ref in zip(perm_refs, xg_refs[1:]):
        xg_ref[0] = _dot(perm_ref[...], yb).astype(bf16).reshape(xg_ref.shape[1:])


def _strided_spec(tile, dilation, width, tiles_per_seq):
    return pl.BlockSpec((1, dilation, tile // dilation, width),
                        lambda i: (i // tiles_per_seq, 0, lax.rem(i, tiles_per_seq), 0))


def _ffn_layer(x, w1, w3, w2, ln_g, ln_b, batch):
    n = x.shape[0]
    seq = n // batch
    d_ff = w1.shape[1]
    row_spec = pl.BlockSpec((FFN_TM, D_MODEL), lambda i: (i, 0))
    dils = [dil for _, dil in DILATED_PAIRS[1:]]
    perms = [jnp.asarray(_stride_perm(FFN_TM, dil), bf16) for dil in dils]
    return pl.pallas_call(
        _ffn_kernel,
        grid=(n // FFN_TM,),
        in_specs=[row_spec, _const_spec((D_MODEL, d_ff)), _const_spec((D_MODEL, d_ff)),
                  _const_spec((d_ff, D_MODEL)), _const_spec((1, D_MODEL)), _const_spec((1, D_MODEL))]
                 + [_const_spec((FFN_TM, FFN_TM)) for _ in dils],
        out_specs=[row_spec, row_spec] + [_strided_spec(FFN_TM, dil, D_MODEL, seq // FFN_TM) for dil in dils],
        out_shape=[jax.ShapeDtypeStruct((n, D_MODEL), f32), jax.ShapeDtypeStruct((n, D_MODEL), bf16)]
                  + [jax.ShapeDtypeStruct((batch, dil, seq // dil, D_MODEL), bf16) for dil in dils],
        compiler_params=_cparams(("arbitrary",)),
        name="dense_ffn",
    )(x, w1, w3, w2, ln_g, ln_b, *perms)


def _qkv_kernel(x_ref, w_ref, o_ref):
    scale = jnp.where(pl.program_id(1) == 0, 1.0 / math.sqrt(HEAD_DIM), 1.0).astype(f32)
    o_ref[...] = (_dot(x_ref[...], w_ref[...]) * scale).astype(o_ref.dtype)


def _qkv_proj(xg, wg):
    n = xg.shape[0]
    return pl.pallas_call(
        _qkv_kernel,
        grid=(n // QKV_TM, 3),
        in_specs=[pl.BlockSpec((QKV_TM, D_MODEL), lambda i, j: (i, 0)),
                  pl.BlockSpec((D_MODEL, D_MODEL), lambda i, j: (0, j))],
        out_specs=pl.BlockSpec((QKV_TM, D_MODEL), lambda i, j: (i, j)),
        out_shape=jax.ShapeDtypeStruct((n, 3 * D_MODEL), bf16),
        compiler_params=_cparams(("arbitrary", "arbitrary")),
        name="qkv_proj",
    )(xg, wg)


def _t5_bucket(dist):
    max_exact = N_BUCKETS // 2
    d = np.maximum(dist, 0)
    large = max_exact + (np.log(np.maximum(d, 1) / max_exact) / math.log(MAX_DISTANCE / max_exact)
                         * (N_BUCKETS - max_exact)).astype(np.int64)
    large = np.minimum(large, N_BUCKETS - 1)
    return np.where(d < max_exact, d, large).astype(np.int32)


def _attn_bias(rel_bias):
    L = ATTN_BLOCK
    qi = np.arange(L)[:, None]
    ki = np.arange(2 * L)[None, :]
    off = qi + L - ki
    out = []
    for window, dilation in DILATED_PAIRS:
        band = (off >= 0) & (off <= window // dilation)
        bucket = _t5_bucket(np.clip(off, 0, None) * dilation)
        bias = jnp.transpose(rel_bias[bucket], (2, 0, 1)).astype(f32)
        out.append(jnp.where(band[None], bias, NEG_INF))
    return jnp.stack(out, axis=0)


def _attn_kernel(q_ref, kp_ref, kc_ref, vp_ref, vc_ref, bias_ref, o_ref, lse_ref, *, blocks_per_seq):
    L = ATTN_BLOCK
    lane = lax.broadcasted_iota(jnp.int32, (L, LANE), 1)
    low = lane < HEAD_DIM
    is_first = lax.rem(pl.program_id(0) * ATT_QB, blocks_per_seq) == 0
    col = lax.broadcasted_iota(jnp.int32, (L, 2 * L), 1)
    pen = jnp.where((col < L) & is_first, NEG_INF, 0.0).astype(f32)
    for s in range(ATT_QB):
        rs = slice(s * L, (s + 1) * L)
        q = q_ref[rs, :]
        if s == 0:
            k = jnp.concatenate([kp_ref[...], kc_ref[rs, :]], axis=0)
            v = jnp.concatenate([vp_ref[...], vc_ref[rs, :]], axis=0)
        else:
            ks = slice((s - 1) * L, (s + 1) * L)
            k = kc_ref[ks, :]
            v = vc_ref[ks, :]
        lse_acc = jnp.zeros((L, LANE), f32)
        for hp in range(N_HEADS // 2):
            cs = slice(hp * LANE, (hp + 1) * LANE)
            qp, kpair, vpair = q[:, cs], k[:, cs], v[:, cs]
            outs = []
            for half in range(2):
                h = 2 * hp + half
                qm = jnp.where(low if half == 0 else jnp.logical_not(low), qp, jnp.zeros_like(qp))
                logits = lax.dot_general(qm, kpair, (((1,), (1,)), ((), ())),
                                         preferred_element_type=f32)
                logits = logits + bias_ref[h]
                if s == 0:
                    logits = logits + pen
                m = jnp.max(logits, axis=-1, keepdims=True)
                p = jnp.exp(logits - m)
                ssum = jnp.sum(p, axis=-1, keepdims=True)
                outs.append(_dot(p.astype(bf16), vpair) / ssum)
                lse_acc = jnp.where(lane == h, m + jnp.log(ssum), lse_acc)
            o_ref[rs, cs] = jnp.where(low, outs[0], outs[1]).astype(o_ref.dtype)
        lse_ref[rs, :] = lse_acc


def _attention(qkv, bias, blocks_per_seq):
    n = qkv.shape[0]
    L = ATTN_BLOCK
    tq = ATT_QB * L
    assert blocks_per_seq % ATT_QB == 0

    def cur(col):
        return pl.BlockSpec((tq, D_MODEL), lambda i: (i, col))

    def prev(col):
        return pl.BlockSpec((L, D_MODEL), lambda i: (jnp.maximum(i * ATT_QB - 1, 0), col))

    return pl.pallas_call(
        functools.partial(_attn_kernel, blocks_per_seq=blocks_per_seq),
        grid=(n // tq,),
        in_specs=[cur(0), prev(1), cur(1), prev(2), cur(2), _const_spec((N_HEADS, L, 2 * L))],
        out_specs=[pl.BlockSpec((tq, D_MODEL), lambda i: (i, 0)), pl.BlockSpec((tq, LANE), lambda i: (i, 0))],
        out_shape=[jax.ShapeDtypeStruct((n, D_MODEL), bf16), jax.ShapeDtypeStruct((n, LANE), f32)],
        compiler_params=_cparams(("arbitrary",)),
        name="dilated_attn",
    )(qkv, qkv, qkv, qkv, qkv, bias)


def _split_bf16(v):
    hi = v.astype(bf16)
    return hi, (v - hi.astype(f32)).astype(bf16)


def _merge_kernel(*refs):
    ng = N_DIL_GROUPS
    o_refs, lse_refs, perm_refs = refs[:ng], refs[ng:2 * ng], refs[2 * ng:3 * ng - 1]
    (x_ref, e_ref, wo_ref, g_ref, b_ref, wrh_ref, wrl_ref, x3_ref, gate_ref, sel_ref) = refs[3 * ng - 1:]
    os_ = [o_refs[0][...].astype(f32)]
    ls = [lse_refs[0][...]]
    for o_ref, lse_ref, perm_ref in zip(o_refs[1:], lse_refs[1:], perm_refs):
        os_.append(_dot(perm_ref[...], o_ref[0].reshape(MRG_TM, D_MODEL)))
        ls.append(_permute_rows_f32(perm_ref[...], lse_ref[0].reshape(MRG_TM, LANE)))
    mx = functools.reduce(jnp.maximum, ls)
    es = [jnp.exp(l - mx) for l in ls]
    den = functools.reduce(jnp.add, es)
    merged = None
    for g in range(ng):
        wh, wl = _split_bf16(es[g] / den)
        wfull = _dot(wh, e_ref[...]) + _dot(wl, e_ref[...])
        term = wfull * os_[g]
        merged = term if merged is None else merged + term
    att = _dot(merged.astype(bf16), wo_ref[...])
    x3 = _layer_norm_rows(DEEPNORM_ALPHA * x_ref[...] + att, g_ref[...], b_ref[...])
    x3_ref[...] = x3

    xh, xl = _split_bf16(x3)
    logits = _dot(xh, wrh_ref[...]) + _dot(xl, wrh_ref[...]) + _dot(xh, wrl_ref[...])
    lane = lax.broadcasted_iota(jnp.int32, logits.shape, 1)
    neg = -jnp.inf
    logits = jnp.where(lane < N_EXPERTS, logits, neg)
    m1 = jnp.max(logits, axis=-1, keepdims=True)
    i1 = jnp.min(jnp.where(logits == m1, lane, LANE), axis=-1, keepdims=True)
    pick1 = lane == i1
    rest = jnp.where(pick1, neg, logits)
    m2 = jnp.max(rest, axis=-1, keepdims=True)
    i2 = jnp.min(jnp.where(rest == m2, lane, LANE), axis=-1, keepdims=True)
    pick2 = lane == i2
    e2 = jnp.exp(m2 - m1)
    gate_ref[...] = jnp.where(pick1, 1.0 / (1.0 + e2), jnp.where(pick2, e2 / (1.0 + e2), 0.0))
    sel_ref[...] = jnp.where(pick1 | pick2, 1.0, 0.0)


def _merge_layer(os_, lses, x, expand, w_o, ln_g, ln_b, wr_hi, wr_lo, batch):
    n = x.shape[0]
    tiles_per_seq = n // batch // MRG_TM
    row = lambda w: pl.BlockSpec((MRG_TM, w), lambda i: (i, 0))
    dils = [dil for _, dil in DILATED_PAIRS[1:]]
    perms = [jnp.asarray(_stride_perm(MRG_TM, dil).T, bf16) for dil in dils]
    return pl.pallas_call(
        _merge_kernel,
        grid=(n // MRG_TM,),
        in_specs=[row(D_MODEL)] + [_strided_spec(MRG_TM, dil, D_MODEL, tiles_per_seq) for dil in dils]
                 + [row(LANE)] + [_strided_spec(MRG_TM, dil, LANE, tiles_per_seq) for dil in dils]
                 + [_const_spec((MRG_TM, MRG_TM)) for _ in dils]
                 + [row(D_MODEL), _const_spec((LANE, D_MODEL)),
                    _const_spec((D_MODEL, D_MODEL)), _const_spec((1, D_MODEL)), _const_spec((1, D_MODEL)),
                    _const_spec((D_MODEL, LANE)), _const_spec((D_MODEL, LANE))],
        out_specs=[row(D_MODEL), row(LANE), row(LANE)],
        out_shape=[jax.ShapeDtypeStruct((n, D_MODEL), f32), jax.ShapeDtypeStruct((n, LANE), f32),
                   jax.ShapeDtypeStruct((n, LANE), f32)],
        compiler_params=_cparams(("arbitrary",)),
        name="attn_merge",
    )(*os_, *lses, *perms, x, expand, w_o, ln_g, ln_b, wr_hi, wr_lo)


def _row_copy(src_hbm, src_row, dst_ref, dst_row, sem):
    return pltpu.make_async_copy(src_hbm.at[pl.ds(src_row, 1), :], dst_ref.at[pl.ds(dst_row, 1), :], sem)


def _dispatch_kernel(pad_start_ref, pad_cnt_ref, ra_ref, rb_ref, x_ref, xs_hbm, zbuf, sem, zsem):
    def fill_copy(e, k):
        return _row_copy(zbuf, 0, xs_hbm, pad_start_ref[e] + k, zsem)

    def tail_copy(k):
        start = pl.multiple_of(pad_start_ref[N_EXPERTS] + k * MOE_TM, MOE_TM)
        return pltpu.make_async_copy(zbuf, xs_hbm.at[pl.ds(start, MOE_TM), :], zsem)

    @pl.when(pl.program_id(0) == 0)
    def _():
        zbuf[...] = jnp.zeros_like(zbuf)
        for e in range(N_EXPERTS):
            lax.fori_loop(0, pad_cnt_ref[e], lambda k, c: (fill_copy(e, k).start(), c)[1], 0)
        lax.fori_loop(0, pad_cnt_ref[N_EXPERTS], lambda k, c: (tail_copy(k).start(), c)[1], 0)

    def issue(r, carry):
        _row_copy(x_ref, r, xs_hbm, ra_ref[0, 0, r], sem).start()
        _row_copy(x_ref, r, xs_hbm, rb_ref[0, 0, r], sem).start()
        return carry

    lax.fori_loop(0, GATHER_TM, issue, 0)

    for _ in range(TOP_K):
        pltpu.make_async_copy(x_ref, xs_hbm.at[pl.ds(0, GATHER_TM), :], sem).wait()

    @pl.when(pl.program_id(0) == 0)
    def _():
        for e in range(N_EXPERTS):
            lax.fori_loop(0, pad_cnt_ref[e], lambda k, c: (fill_copy(e, k).wait(), c)[1], 0)
        lax.fori_loop(0, pad_cnt_ref[N_EXPERTS], lambda k, c: (tail_copy(k).wait(), c)[1], 0)


def _dispatch_rows(pad_start, pad_cnt, ra, rb, x, rows_total):
    n = x.shape[0]
    smem = pl.BlockSpec((1, 1, GATHER_TM), lambda i, ps, pc: (i, 0, 0), memory_space=pltpu.SMEM)
    grid_spec = pltpu.PrefetchScalarGridSpec(
        num_scalar_prefetch=2,
        grid=(n // GATHER_TM,),
        in_specs=[smem, smem, pl.BlockSpec((GATHER_TM, D_MODEL), lambda i, ps, pc: (i, 0))],
        out_specs=pl.BlockSpec(memory_space=pl.ANY),
        scratch_shapes=[pltpu.VMEM((MOE_TM, D_MODEL), x.dtype), pltpu.SemaphoreType.DMA(()),
                        pltpu.SemaphoreType.DMA(())],
    )
    return pl.pallas_call(
        _dispatch_kernel,
        grid_spec=grid_spec,
        out_shape=jax.ShapeDtypeStruct((rows_total, D_MODEL), x.dtype),
        compiler_params=_cparams(("arbitrary",)),
        name="moe_dispatch",
    )(pad_start, pad_cnt, ra, rb, x)


def _moe_kernel(te_ref, act_ref, x_ref, w1_ref, w3_ref, w2_ref, o_ref):
    del te_ref
    i, f = pl.program_id(0), pl.program_id(1)

    @pl.when(act_ref[i] == 1)
    def _():
        xb = x_ref[...].astype(bf16)
        h = jax.nn.silu(_dot(xb, w1_ref[0])) * _dot(xb, w3_ref[0])
        part = _dot(h.astype(bf16), w2_ref[0])

        @pl.when(f == 0)
        def _():
            o_ref[...] = part

        @pl.when(f > 0)
        def _():
            o_ref[...] = o_ref[...] + part

    @pl.when((act_ref[i] == 0) & (f == 0))
    def _():
        o_ref[...] = jnp.zeros_like(o_ref)


def _moe_gmm(tile_expert, tile_active, xs, w1, w3, w2):
    n_tiles = tile_expert.shape[0]
    d_ff = w1.shape[2]
    grid_spec = pltpu.PrefetchScalarGridSpec(
        num_scalar_prefetch=2,
        grid=(n_tiles, d_ff // MOE_TF),
        in_specs=[pl.BlockSpec((MOE_TM, D_MODEL), lambda i, f, te, act: (i * act[i], 0)),
                  pl.BlockSpec((1, D_MODEL, MOE_TF), lambda i, f, te, act: (te[i], 0, f)),
                  pl.BlockSpec((1, D_MODEL, MOE_TF), lambda i, f, te, act: (te[i], 0, f)),
                  pl.BlockSpec((1, MOE_TF, D_MODEL), lambda i, f, te, act: (te[i], f, 0))],
        out_specs=pl.BlockSpec((MOE_TM, D_MODEL), lambda i, f, te, act: (i, 0)),
    )
    return pl.pallas_call(
        _moe_kernel,
        grid_spec=grid_spec,
        out_shape=jax.ShapeDtypeStruct((n_tiles * MOE_TM, D_MODEL), f32),
        compiler_params=_cparams(("arbitrary", "arbitrary")),
        name="moe_gmm",
    )(tile_expert, tile_active, xs, w1, w3, w2)


def _combine_kernel(ra_ref, rb_ref, ra_next, rb_next, ga_ref, gb_ref, x_ref, y_hbm, g_ref, b_ref, o_ref,
                    ybuf, sem):
    i = pl.program_id(0)
    slot = lax.rem(i, 2)

    def gather(ia_ref, ib_ref, s):
        def issue(r, carry):
            _row_copy(y_hbm, ia_ref[0, 0, r], ybuf.at[s, 0], r, sem.at[s, 0]).start()
            _row_copy(y_hbm, ib_ref[0, 0, r], ybuf.at[s, 1], r, sem.at[s, 1]).start()
            return carry

        lax.fori_loop(0, COMBINE_TM, issue, 0)

    @pl.when(i == 0)
    def _():
        gather(ra_ref, rb_ref, 0)

    @pl.when(i + 1 < pl.num_programs(0))
    def _():
        gather(ra_next, rb_next, 1 - slot)

    for k in range(TOP_K):
        pltpu.make_async_copy(y_hbm.at[pl.ds(0, COMBINE_TM), :], ybuf.at[slot, k], sem.at[slot, k]).wait()
    f = ga_ref[...] * ybuf[slot, 0] + gb_ref[...] * ybuf[slot, 1]
    o_ref[...] = _layer_norm_rows(DEEPNORM_ALPHA * x_ref[...] + f, g_ref[...], b_ref[...])


def _combine(ra, rb, ga, gb, x, ys, ln_g, ln_b):
    n = x.shape[0]
    last = n // COMBINE_TM - 1
    smem = pl.BlockSpec((1, 1, COMBINE_TM), lambda i: (i, 0, 0), memory_space=pltpu.SMEM)
    smem_next = pl.BlockSpec((1, 1, COMBINE_TM), lambda i: (jnp.minimum(i + 1, last), 0, 0),
                             memory_space=pltpu.SMEM)
    col = pl.BlockSpec((COMBINE_TM, 1), lambda i: (i, 0))
    row = pl.BlockSpec((COMBINE_TM, D_MODEL), lambda i: (i, 0))
    return pl.pallas_call(
        _combine_kernel,
        grid=(n // COMBINE_TM,),
        in_specs=[smem, smem, smem_next, smem_next, col, col, row, pl.BlockSpec(memory_space=pl.ANY),
                  _const_spec((1, D_MODEL)), _const_spec((1, D_MODEL))],
        out_specs=row,
        out_shape=jax.ShapeDtypeStruct((n, D_MODEL), f32),
        scratch_shapes=[pltpu.VMEM((2, TOP_K, COMBINE_TM, D_MODEL), f32), pltpu.SemaphoreType.DMA((2, TOP_K))],
        compiler_params=_cparams(("arbitrary",)),
        name="moe_combine",
    )(ra, rb, ra, rb, ga, gb, x, ys, ln_g, ln_b)


def _moe_plan(sel, gates):
    n = sel.shape[0]
    rows_total = TOP_K * n + N_EXPERTS * MOE_TM
    n_tiles = rows_total // MOE_TM
    seli = sel.astype(jnp.int32)
    cnt = jnp.sum(seli, axis=0)
    pos = jnp.cumsum(seli, axis=0) - seli
    padded = ((cnt + MOE_TM - 1) // MOE_TM) * MOE_TM
    bounds = jnp.cumsum(padded)
    row = (bounds - padded)[None, :] + pos
    pad_start = jnp.concatenate([bounds - padded + cnt, bounds[-1:]]).astype(jnp.int32)
    pad_cnt = jnp.concatenate([padded - cnt, (rows_total - bounds[-1:]) // MOE_TM]).astype(jnp.int32)
    tile_start = jnp.arange(n_tiles, dtype=jnp.int32) * MOE_TM
    last_used = jnp.max(jnp.where(cnt > 0, jnp.arange(N_EXPERTS, dtype=jnp.int32), 0))
    tile_expert = jnp.minimum(jnp.searchsorted(bounds, tile_start, side='right').astype(jnp.int32), last_used)
    tile_active = (tile_start < bounds[-1]).astype(jnp.int32)
    eid = jnp.arange(N_EXPERTS, dtype=jnp.int32)[None, :]
    ea = jnp.min(jnp.where(sel, eid, N_EXPERTS), axis=1, keepdims=True)
    eb = jnp.max(jnp.where(sel, eid, -1), axis=1, keepdims=True)
    take = lambda a, e: jnp.take_along_axis(a, e, axis=1)
    return (pad_start, pad_cnt, tile_expert, tile_active,
            take(row, ea)[:, 0].astype(jnp.int32), take(row, eb)[:, 0].astype(jnp.int32),
            take(gates, ea), take(gates, eb))


def kernel(x, s5_w_in, s5_a_re, s5_a_im, s5_log_dt, s5_b_re, s5_b_im, s5_c_re, s5_c_im, s5_d, s5_w_glu,
           attn_w_qkv, attn_w_o, rel_bias, ffn_w1, ffn_w3, ffn_w2, moe_w_router, moe_w1, moe_w3, moe_w2,
           ln_g, ln_b):
    batch, seq, d = x.shape
    n = batch * seq
    assert d == D_MODEL
    assert seq % (DILATED_PAIRS[-1][1] * ATTN_BLOCK * ATT_QB) == 0
    assert seq % S5_TB == 0 and seq % FFN_TM == 0 and FFN_TM == MRG_TM
    vec = lambda v: v.reshape(1, D_MODEL).astype(f32)

    bre, bim, cre, cimn, tab = _s5_tables(s5_a_re[0], s5_a_im[0], s5_log_dt[0], s5_b_re[0], s5_b_im[0],
                                          s5_c_re[0], s5_c_im[0])
    x1 = _s5_layer(x.reshape(n, d), s5_w_in[0].astype(bf16), bre, bim, cre, cimn, tab, vec(s5_d[0]),
                   s5_w_glu[0].astype(bf16), vec(ln_g[0, 0]), vec(ln_b[0, 0]), batch)
    x2, *xgs = _ffn_layer(x1, ffn_w1[0].astype(bf16), ffn_w3[0].astype(bf16), ffn_w2[0].astype(bf16),
                          vec(ln_g[0, 1]), vec(ln_b[0, 1]), batch)

    wq = attn_w_qkv[0].reshape(d, N_DIL_GROUPS, 3 * d).astype(bf16)
    bias = _attn_bias(rel_bias)
    os_, lses = [], []
    for g, (_, dil) in enumerate(DILATED_PAIRS):
        qkv = _qkv_proj(xgs[g].reshape(n, d), wq[:, g])
        o, lse = _attention(qkv, bias[g], seq // (dil * ATTN_BLOCK))
        os_.append(o if g == 0 else o.reshape(batch, dil, seq // dil, d))
        lses.append(lse if g == 0 else lse.reshape(batch, dil, seq // dil, LANE))
    expand = (jnp.arange(LANE)[:, None] == (jnp.arange(D_MODEL)[None, :] // HEAD_DIM)).astype(bf16)
    wr = jnp.pad(moe_w_router[0].astype(f32), ((0, 0), (0, LANE - N_EXPERTS)))
    wr_hi, wr_lo = _split_bf16(wr)
    x3, gates, sel = _merge_layer(os_, lses, x2, expand, attn_w_o[0].astype(bf16), vec(ln_g[1, 0]),
                                  vec(ln_b[1, 0]), wr_hi, wr_lo, batch)

    pad_start, pad_cnt, tile_expert, tile_active, ra, rb, ga, gb = _moe_plan(sel[:, :N_EXPERTS] > 0.5,
                                                                             gates[:, :N_EXPERTS])
    xs = _dispatch_rows(pad_start, pad_cnt, ra.reshape(-1, 1, GATHER_TM), rb.reshape(-1, 1, GATHER_TM), x3,
                        TOP_K * n + N_EXPERTS * MOE_TM)
    ys = _moe_gmm(tile_expert, tile_active, xs, moe_w1[0].astype(bf16), moe_w3[0].astype(bf16),
                  moe_w2[0].astype(bf16))
    out = _combine(ra.reshape(-1, 1, COMBINE_TM), rb.reshape(-1, 1, COMBINE_TM), ga, gb, x3, ys,
                   vec(ln_g[1, 1]), vec(ln_b[1, 1]))
    return out.reshape(batch, seq, d)
```

```python
import functools
import math

import numpy as np
import jax
import jax.numpy as jnp
from jax import lax
from jax.experimental import pallas as pl
from jax.experimental.pallas import tpu as pltpu

D_MODEL = 1024
S5_GROUP = 16
S5_GROUPS = D_MODEL // S5_GROUP
S5_STATE = 64
S5_WIDTH = S5_GROUPS * S5_STATE
HEAD_DIM = 64
N_HEADS = D_MODEL // HEAD_DIM
DILATED_PAIRS = ((128, 1), (512, 4), (2048, 16))
N_DIL_GROUPS = len(DILATED_PAIRS)
ATTN_BLOCK = 128
N_BUCKETS = 32
MAX_DISTANCE = 2048
N_EXPERTS = 8
TOP_K = 2
DEPTH = 2
DEEPNORM_ALPHA = (2 * DEPTH) ** 0.25
LN_EPS = 1e-5
NEG_INF = -1e30
LOG2_E = math.log2(math.e)

LANE = 128
SUBLANE = 8
BF16_SUBLANES = 16
MXU_DIM = 256
VMEM_LIMIT = 56 * 1024 * 1024

S5_T = 32
S5_TB = SUBLANE * S5_T
S5_LC = 512
S5_KT = MXU_DIM
S5_NT = D_MODEL // S5_KT
S5_ST = S5_WIDTH // S5_NT
FFN_TM = 512
QKV_TM = 1024
ATT_QB = 4
MRG_TM = 512
MOE_TM = 512
MOE_TF = 1792
GATHER_TM = 512
COMBINE_TM = 256
ISSUE_UNROLL = 8

f32 = jnp.float32
bf16 = jnp.bfloat16


def _cparams(sem):
    return pltpu.CompilerParams(dimension_semantics=sem, vmem_limit_bytes=VMEM_LIMIT)


def _const_spec(shape):
    nd = len(shape)
    return pl.BlockSpec(shape, lambda *_: (0,) * nd, pipeline_mode=pl.Buffered(1))


def _layer_norm_rows(v, g, b):
    mu = jnp.mean(v, axis=-1, keepdims=True)
    c = v - mu
    var = jnp.mean(c * c, axis=-1, keepdims=True)
    return c * lax.rsqrt(var + LN_EPS) * g + b


def _gelu_tanh(x):
    return 0.5 * x * (1.0 + jnp.tanh(math.sqrt(2.0 / math.pi) * (x + 0.044715 * (x * x * x))))


def _dot(a, b):
    return jnp.dot(a, b, preferred_element_type=f32)


_TAB_A = 0
_TAB_P1 = 2
_TAB_P2 = 4
_TAB_P4 = 6
_TAB_PC_RE = 8
_TAB_PC_IM = 16
_TAB_ROWS = 24


def _split3_bf16(v):
    hi = v.astype(bf16)
    r1 = v - hi.astype(f32)
    mid = r1.astype(bf16)
    return hi, mid, (r1 - mid.astype(f32)).astype(bf16)


def _permute_rows_f32(perm, v):
    hi, mid, lo = _split3_bf16(v)
    return _dot(perm, hi) + _dot(perm, mid) + _dot(perm, lo)


def _s5_kernel(x_ref, pf_ref, pb_ref, win_ref, bre_ref, bim_ref, cre_ref, cimn_ref, tab_ref, d_ref, wglu_ref,
               g_ref, b_ref, o_ref, u_ref, hre_ref, him_ref, car_re, car_im):
    T = S5_T

    @pl.when(pl.program_id(1) == 0)
    def _():
        car_re[...] = jnp.zeros_like(car_re)
        car_im[...] = jnp.zeros_like(car_im)

    x = x_ref[...]
    xs = _dot(pf_ref[...], x.astype(bf16)).astype(bf16)
    u_ref[...] = _dot(xs, win_ref[...])

    for kt in range(S5_NT):
        ub = u_ref[:, kt * S5_KT:(kt + 1) * S5_KT].astype(bf16)
        hre_ref[:, kt * S5_ST:(kt + 1) * S5_ST] = _dot(ub, bre_ref[kt])
        him_ref[:, kt * S5_ST:(kt + 1) * S5_ST] = _dot(ub, bim_ref[kt])

    row = lax.broadcasted_iota(jnp.int32, (SUBLANE, S5_LC), 0)

    def bcast(r, sl):
        return jnp.broadcast_to(tab_ref[r:r + 1, sl], (SUBLANE, S5_LC))

    def cmul(ar, ai, xr, xi):
        return ar * xr - ai * xi, ar * xi + ai * xr

    for lc in range(S5_WIDTH // S5_LC):
        sl = slice(lc * S5_LC, (lc + 1) * S5_LC)
        ar, ai = bcast(_TAB_A, sl), bcast(_TAB_A + 1, sl)

        er, ei = hre_ref[0:SUBLANE, sl], him_ref[0:SUBLANE, sl]
        for t in range(1, T):
            rows = slice(t * SUBLANE, (t + 1) * SUBLANE)
            pr, pi = cmul(ar, ai, er, ei)
            er = pr + hre_ref[rows, sl]
            ei = pi + him_ref[rows, sl]
            hre_ref[rows, sl] = er
            him_ref[rows, sl] = ei

        xr, xi = er, ei
        for s, tr in ((1, _TAB_P1), (2, _TAB_P2), (4, _TAB_P4)):
            sr = jnp.where(row >= s, pltpu.roll(xr, s, 0), 0.0)
            si = jnp.where(row >= s, pltpu.roll(xi, s, 0), 0.0)
            mr, mi = cmul(bcast(tr, sl), bcast(tr + 1, sl), sr, si)
            xr, xi = xr + mr, xi + mi
        shr = jnp.where(row >= 1, pltpu.roll(xr, 1, 0), 0.0)
        shi = jnp.where(row >= 1, pltpu.roll(xi, 1, 0), 0.0)
        cr, ci = car_re[:, sl], car_im[:, sl]
        pcr = tab_ref[_TAB_PC_RE:_TAB_PC_RE + SUBLANE, sl]
        pci = tab_ref[_TAB_PC_IM:_TAB_PC_IM + SUBLANE, sl]
        hr0, hi0 = cmul(pcr, pci, cr, ci)
        hr0, hi0 = hr0 + shr, hi0 + shi
        nr, ni = cmul(bcast(_TAB_P1, sl), bcast(_TAB_P1 + 1, sl), hr0, hi0)
        nr, ni = nr + er, ni + ei
        car_re[:, sl] = jnp.broadcast_to(nr[SUBLANE - 1:SUBLANE], (SUBLANE, S5_LC))
        car_im[:, sl] = jnp.broadcast_to(ni[SUBLANE - 1:SUBLANE], (SUBLANE, S5_LC))

        zr, zi = hr0, hi0
        for t in range(T):
            rows = slice(t * SUBLANE, (t + 1) * SUBLANE)
            zr, zi = cmul(ar, ai, zr, zi)
            hre_ref[rows, sl] = hre_ref[rows, sl] + zr
            him_ref[rows, sl] = him_ref[rows, sl] + zi

    ys = []
    for nt in range(S5_NT):
        hr = hre_ref[:, nt * S5_ST:(nt + 1) * S5_ST].astype(bf16)
        hi = him_ref[:, nt * S5_ST:(nt + 1) * S5_ST].astype(bf16)
        csl = slice(nt * S5_KT, (nt + 1) * S5_KT)
        ys.append(_dot(hr, cre_ref[nt]) + _dot(hi, cimn_ref[nt]) + d_ref[:, csl] * u_ref[:, csl])
    y = jnp.concatenate(ys, axis=1)
    vg = _dot(_gelu_tanh(y).astype(bf16), wglu_ref[...])
    h = vg[:, :D_MODEL] * jax.nn.sigmoid(vg[:, D_MODEL:])
    h = _permute_rows_f32(pb_ref[...], h)
    o_ref[...] = _layer_norm_rows(DEEPNORM_ALPHA * x + h, g_ref[...], b_ref[...])


def _s5_tables(a_re, a_im, log_dt, b_re, b_im, c_re, c_im):
    dt = jnp.exp(log_dt.astype(f32))[:, None]
    lam_re = jnp.minimum(a_re.astype(f32), -1e-4)
    lam_im = a_im.astype(f32)
    mag = jnp.exp(lam_re * dt)
    ab_re = mag * jnp.cos(lam_im * dt)
    ab_im = mag * jnp.sin(lam_im * dt)
    den = lam_re * lam_re + lam_im * lam_im
    nr = ab_re - 1.0
    q_re = (nr * lam_re + ab_im * lam_im) / den
    q_im = (ab_im * lam_re - nr * lam_im) / den
    br, bi = b_re.astype(f32), b_im.astype(f32)
    bb_re = q_re[..., None] * br - q_im[..., None] * bi
    bb_im = q_re[..., None] * bi + q_im[..., None] * br

    gpt = S5_KT // S5_GROUP
    eye = jnp.eye(gpt, dtype=f32)

    def b_blocks(bb):
        t = bb.reshape(S5_NT, gpt, S5_STATE, S5_GROUP)
        blk = jnp.einsum('ngpc,gh->ngchp', t, eye)
        return blk.reshape(S5_NT, S5_KT, S5_ST).astype(bf16)

    def c_blocks(cc):
        t = cc.reshape(S5_NT, gpt, S5_GROUP, S5_STATE)
        blk = jnp.einsum('ngcp,gh->ngphc', t, eye)
        return blk.reshape(S5_NT, S5_ST, S5_KT).astype(bf16)

    def csq(z):
        return z[0] * z[0] - z[1] * z[1], 2.0 * z[0] * z[1]

    def cmul(x, y):
        return x[0] * y[0] - x[1] * y[1], x[0] * y[1] + x[1] * y[0]

    a1 = (ab_re.reshape(-1), ab_im.reshape(-1))
    p = a1
    for _ in range(int(math.log2(S5_T))):
        p = csq(p)
    p1 = p
    p2 = csq(p1)
    p4 = csq(p2)
    pc = [(jnp.ones_like(p1[0]), jnp.zeros_like(p1[0]))]
    for _ in range(SUBLANE - 1):
        pc.append(cmul(pc[-1], p1))
    rows = [a1[0], a1[1], p1[0], p1[1], p2[0], p2[1], p4[0], p4[1]]
    rows += [z[0] for z in pc] + [z[1] for z in pc]
    tab = jnp.stack(rows, axis=0)
    return (b_blocks(bb_re), b_blocks(bb_im), c_blocks(c_re.astype(f32)), c_blocks(-c_im.astype(f32)), tab)


def _scan_perm():
    p = np.zeros((S5_TB, S5_TB), np.float32)
    for c in range(SUBLANE):
        for t in range(S5_T):
            p[t * SUBLANE + c, c * S5_T + t] = 1.0
    return p


def _perm_rows(dilation):
    return max(LANE, BF16_SUBLANES * dilation)


def _stride_perm(tile, dilation):
    p = np.zeros((tile, tile), np.float32)
    per = tile // dilation
    for j in range(per):
        for r in range(dilation):
            p[r * per + j, j * dilation + r] = 1.0
    return p


def _s5_layer(x, w_in, bre, bim, cre, cimn, tab, d_skip, w_glu, ln_g, ln_b, batch):
    n = x.shape[0]
    steps = n // batch // S5_TB
    row_spec = pl.BlockSpec((S5_TB, D_MODEL), lambda b, t: (b * steps + t, 0))
    perm = _scan_perm()
    return pl.pallas_call(
        _s5_kernel,
        grid=(batch, steps),
        in_specs=[row_spec,
                  _const_spec((S5_TB, S5_TB)), _const_spec((S5_TB, S5_TB)),
                  _const_spec((D_MODEL, D_MODEL)),
                  _const_spec((S5_NT, S5_KT, S5_ST)), _const_spec((S5_NT, S5_KT, S5_ST)),
                  _const_spec((S5_NT, S5_ST, S5_KT)), _const_spec((S5_NT, S5_ST, S5_KT)),
                  _const_spec((_TAB_ROWS, S5_WIDTH)),
                  _const_spec((1, D_MODEL)),
                  _const_spec((D_MODEL, 2 * D_MODEL)),
                  _const_spec((1, D_MODEL)), _const_spec((1, D_MODEL))],
        out_specs=row_spec,
        out_shape=jax.ShapeDtypeStruct((n, D_MODEL), f32),
        scratch_shapes=[pltpu.VMEM((S5_TB, D_MODEL), f32),
                        pltpu.VMEM((S5_TB, S5_WIDTH), f32), pltpu.VMEM((S5_TB, S5_WIDTH), f32),
                        pltpu.VMEM((SUBLANE, S5_WIDTH), f32), pltpu.VMEM((SUBLANE, S5_WIDTH), f32)],
        compiler_params=_cparams(("arbitrary", "arbitrary")),
        name="s5_mixer",
    )(x, jnp.asarray(perm, bf16), jnp.asarray(perm.T, bf16), w_in, bre, bim, cre, cimn, tab, d_skip, w_glu,
      ln_g, ln_b)


def _ffn_kernel(x_ref, w1_ref, w3_ref, w2_ref, g_ref, b_ref, *rest):
    perm_refs = rest[:N_DIL_GROUPS - 1]
    o_ref = rest[N_DIL_GROUPS - 1]
    xg_refs = rest[N_DIL_GROUPS:]
    x = x_ref[...]
    xb = x.astype(bf16)
    h = jax.nn.silu(_dot(xb, w1_ref[...])) * _dot(xb, w3_ref[...])
    f = _dot(h.astype(bf16), w2_ref[...])
    y = _layer_norm_rows(DEEPNORM_ALPHA * x + f, g_ref[...], b_ref[...])
    o_ref[...] = y
    yb = y.astype(bf16)
    xg_refs[0][...] = yb
    for perm_ref, xg_ref in zip(perm_refs, xg_refs[1:]):
        dil = xg_ref.shape[1]
        blk = perm_ref.shape[0]
        piece = blk // dil
        for s in range(FFN_TM // blk):
            v = _dot(perm_ref[...], yb[s * blk:(s + 1) * blk]).astype(bf16)
            xg_ref[0, :, s * piece:(s + 1) * piece, :] = v.reshape(dil, piece, D_MODEL)


def _strided_spec(tile, dilation, width, tiles_per_seq):
    return pl.BlockSpec((1, dilation, tile // dilation, width),
                        lambda i: (i // tiles_per_seq, 0, lax.rem(i, tiles_per_seq), 0))


def _ffn_layer(x, w1, w3, w2, ln_g, ln_b, batch):
    n = x.shape[0]
    seq = n // batch
    d_ff = w1.shape[1]
    row_spec = pl.BlockSpec((FFN_TM, D_MODEL), lambda i: (i, 0))
    dils = [dil for _, dil in DILATED_PAIRS[1:]]
    perms = [jnp.asarray(_stride_perm(_perm_rows(dil), dil), bf16) for dil in dils]
    return pl.pallas_call(
        _ffn_kernel,
        grid=(n // FFN_TM,),
        in_specs=[row_spec, _const_spec((D_MODEL, d_ff)), _const_spec((D_MODEL, d_ff)),
                  _const_spec((d_ff, D_MODEL)), _const_spec((1, D_MODEL)), _const_spec((1, D_MODEL))]
                 + [_const_spec(p.shape) for p in perms],
        out_specs=[row_spec, row_spec] + [_strided_spec(FFN_TM, dil, D_MODEL, seq // FFN_TM) for dil in dils],
        out_shape=[jax.ShapeDtypeStruct((n, D_MODEL), f32), jax.ShapeDtypeStruct((n, D_MODEL), bf16)]
                  + [jax.ShapeDtypeStruct((batch, dil, seq // dil, D_MODEL), bf16) for dil in dils],
        compiler_params=_cparams(("arbitrary",)),
        name="dense_ffn",
    )(x, w1, w3, w2, ln_g, ln_b, *perms)


def _qkv_kernel(x_ref, w_ref, o_ref):
    scale = jnp.where(pl.program_id(1) == 0, LOG2_E / math.sqrt(HEAD_DIM), 1.0).astype(f32)
    o_ref[...] = (_dot(x_ref[...], w_ref[...]) * scale).astype(o_ref.dtype)


def _qkv_proj(xg, wg):
    n = xg.shape[0]
    return pl.pallas_call(
        _qkv_kernel,
        grid=(n // QKV_TM, 3),
        in_specs=[pl.BlockSpec((QKV_TM, D_MODEL), lambda i, j: (i, 0)),
                  pl.BlockSpec((D_MODEL, D_MODEL), lambda i, j: (0, j))],
        out_specs=pl.BlockSpec((QKV_TM, D_MODEL), lambda i, j: (i, j)),
        out_shape=jax.ShapeDtypeStruct((n, 3 * D_MODEL), bf16),
        compiler_params=_cparams(("arbitrary", "arbitrary")),
        name="qkv_proj",
    )(xg, wg)


def _t5_bucket(dist):
    max_exact = N_BUCKETS // 2
    d = np.maximum(dist, 0)
    large = max_exact + (np.log(np.maximum(d, 1) / max_exact) / math.log(MAX_DISTANCE / max_exact)
                         * (N_BUCKETS - max_exact)).astype(np.int64)
    large = np.minimum(large, N_BUCKETS - 1)
    return np.where(d < max_exact, d, large).astype(np.int32)


def _attn_bias(rel_bias):
    L = ATTN_BLOCK
    qi = np.arange(L)[:, None]
    ki = np.arange(2 * L)[None, :]
    off = qi + L - ki
    out = []
    for window, dilation in DILATED_PAIRS:
        band = (off >= 0) & (off <= window // dilation)
        bucket = _t5_bucket(np.clip(off, 0, None) * dilation)
        bias = jnp.transpose(rel_bias[bucket], (2, 0, 1)).astype(f32)
        out.append(jnp.where(band[None], bias * LOG2_E, NEG_INF))
    return jnp.stack(out, axis=0)


def _attn_kernel(q_ref, kp_ref, kc_ref, vp_ref, vc_ref, bias_ref, o_ref, lse_ref, *, blocks_per_seq):
    L = ATTN_BLOCK
    lane = lax.broadcasted_iota(jnp.int32, (L, LANE), 1)
    low = lane < HEAD_DIM
    is_first = lax.rem(pl.program_id(0) * ATT_QB, blocks_per_seq) == 0
    col = lax.broadcasted_iota(jnp.int32, (L, 2 * L), 1)
    pen = jnp.where((col < L) & is_first, NEG_INF, 0.0).astype(f32)
    for s in range(ATT_QB):
        rs = slice(s * L, (s + 1) * L)
        q = q_ref[rs, :]
        if s == 0:
            k = jnp.concatenate([kp_ref[...], kc_ref[rs, :]], axis=0)
            v = jnp.concatenate([vp_ref[...], vc_ref[rs, :]], axis=0)
        else:
            ks = slice((s - 1) * L, (s + 1) * L)
            k = kc_ref[ks, :]
            v = vc_ref[ks, :]
        lse_acc = jnp.zeros((L, LANE), f32)
        for hp in range(N_HEADS // 2):
            cs = slice(hp * LANE, (hp + 1) * LANE)
            qp, kpair, vpair = q[:, cs], k[:, cs], v[:, cs]
            outs = []
            for half in range(2):
                h = 2 * hp + half
                qm = jnp.where(low if half == 0 else jnp.logical_not(low), qp, jnp.zeros_like(qp))
                logits = lax.dot_general(qm, kpair, (((1,), (1,)), ((), ())),
                                         preferred_element_type=f32)
                logits = logits + bias_ref[h]
                if s == 0:
                    logits = logits + pen
                m = jnp.max(logits, axis=-1, keepdims=True)
                p = jnp.exp2(logits - m)
                ssum = jnp.sum(p, axis=-1, keepdims=True)
                outs.append(_dot(p.astype(bf16), vpair) / ssum)
                lse_acc = jnp.where(lane == h, (m + jnp.log2(ssum)) * math.log(2.0), lse_acc)
            o_ref[rs, cs] = jnp.where(low, outs[0], outs[1]).astype(o_ref.dtype)
        lse_ref[rs, :] = lse_acc


def _attention(qkv, bias, blocks_per_seq):
    n = qkv.shape[0]
    L = ATTN_BLOCK
    tq = ATT_QB * L
    assert blocks_per_seq % ATT_QB == 0

    def cur(col):
        return pl.BlockSpec((tq, D_MODEL), lambda i: (i, col))

    def prev(col):
        return pl.BlockSpec((L, D_MODEL), lambda i: (jnp.maximum(i * ATT_QB - 1, 0), col))

    return pl.pallas_call(
        functools.partial(_attn_kernel, blocks_per_seq=blocks_per_seq),
        grid=(n // tq,),
        in_specs=[cur(0), prev(1), cur(1), prev(2), cur(2), _const_spec((N_HEADS, L, 2 * L))],
        out_specs=[pl.BlockSpec((tq, D_MODEL), lambda i: (i, 0)), pl.BlockSpec((tq, LANE), lambda i: (i, 0))],
        out_shape=[jax.ShapeDtypeStruct((n, D_MODEL), bf16), jax.ShapeDtypeStruct((n, LANE), f32)],
        compiler_params=_cparams(("arbitrary",)),
        name="dilated_attn",
    )(qkv, qkv, qkv, qkv, qkv, bias)


def _split_bf16(v):
    hi = v.astype(bf16)
    return hi, (v - hi.astype(f32)).astype(bf16)


def _unstride_rows(ref, perm_ref, apply):
    dil, per, width = ref.shape[1:]
    blk = perm_ref.shape[0]
    piece = blk // dil
    outs = []
    for s in range(dil * per // blk):
        v = ref[0, :, s * piece:(s + 1) * piece, :].reshape(blk, width)
        outs.append(apply(perm_ref[...], v))
    return jnp.concatenate(outs, axis=0)


def _merge_kernel(*refs):
    ng = N_DIL_GROUPS
    o_refs, lse_refs, perm_refs = refs[:ng], refs[ng:2 * ng], refs[2 * ng:3 * ng - 1]
    (x_ref, e_ref, wo_ref, g_ref, b_ref, wrh_ref, wrl_ref, x3_ref, gate_ref, sel_ref) = refs[3 * ng - 1:]
    os_ = [o_refs[0][...].astype(f32)]
    ls = [lse_refs[0][...]]
    for o_ref, lse_ref, perm_ref in zip(o_refs[1:], lse_refs[1:], perm_refs):
        os_.append(_unstride_rows(o_ref, perm_ref, lambda p, v: _dot(p, v)))
        ls.append(_unstride_rows(lse_ref, perm_ref, _permute_rows_f32))
    mx = functools.reduce(jnp.maximum, ls)
    es = [jnp.exp(l - mx) for l in ls]
    den = functools.reduce(jnp.add, es)
    merged = None
    for g in range(ng):
        wfull = _dot((es[g] / den).astype(bf16), e_ref[...])
        term = wfull * os_[g]
        merged = term if merged is None else merged + term
    att = _dot(merged.astype(bf16), wo_ref[...])
    x3 = _layer_norm_rows(DEEPNORM_ALPHA * x_ref[...] + att, g_ref[...], b_ref[...])
    x3_ref[...] = x3

    xh, xl = _split_bf16(x3)
    logits = _dot(xh, wrh_ref[...]) + _dot(xl, wrh_ref[...]) + _dot(xh, wrl_ref[...])
    lane = lax.broadcasted_iota(jnp.int32, logits.shape, 1)
    neg = -jnp.inf
    logits = jnp.where(lane < N_EXPERTS, logits, neg)
    m1 = jnp.max(logits, axis=-1, keepdims=True)
    i1 = jnp.min(jnp.where(logits == m1, lane, LANE), axis=-1, keepdims=True)
    pick1 = lane == i1
    rest = jnp.where(pick1, neg, logits)
    m2 = jnp.max(rest, axis=-1, keepdims=True)
    i2 = jnp.min(jnp.where(rest == m2, lane, LANE), axis=-1, keepdims=True)
    pick2 = lane == i2
    e2 = jnp.exp(m2 - m1)
    gate_ref[...] = jnp.where(pick1, 1.0 / (1.0 + e2), jnp.where(pick2, e2 / (1.0 + e2), 0.0))
    sel_ref[...] = jnp.where(pick1 | pick2, 1.0, 0.0)


def _merge_layer(os_, lses, x, expand, w_o, ln_g, ln_b, wr_hi, wr_lo, batch):
    n = x.shape[0]
    tiles_per_seq = n // batch // MRG_TM
    row = lambda w: pl.BlockSpec((MRG_TM, w), lambda i: (i, 0))
    dils = [dil for _, dil in DILATED_PAIRS[1:]]
    perms = [jnp.asarray(_stride_perm(_perm_rows(dil), dil).T, bf16) for dil in dils]
    return pl.pallas_call(
        _merge_kernel,
        grid=(n // MRG_TM,),
        in_specs=[row(D_MODEL)] + [_strided_spec(MRG_TM, dil, D_MODEL, tiles_per_seq) for dil in dils]
                 + [row(LANE)] + [_strided_spec(MRG_TM, dil, LANE, tiles_per_seq) for dil in dils]
                 + [_const_spec(p.shape) for p in perms]
                 + [row(D_MODEL), _const_spec((LANE, D_MODEL)),
                    _const_spec((D_MODEL, D_MODEL)), _const_spec((1, D_MODEL)), _const_spec((1, D_MODEL)),
                    _const_spec((D_MODEL, LANE)), _const_spec((D_MODEL, LANE))],
        out_specs=[row(D_MODEL), row(LANE), row(LANE)],
        out_shape=[jax.ShapeDtypeStruct((n, D_MODEL), f32), jax.ShapeDtypeStruct((n, LANE), f32),
                   jax.ShapeDtypeStruct((n, LANE), f32)],
        compiler_params=_cparams(("arbitrary",)),
        name="attn_merge",
    )(*os_, *lses, *perms, x, expand, w_o, ln_g, ln_b, wr_hi, wr_lo)


def _row_copy(src_hbm, src_row, dst_ref, dst_row, sem):
    return pltpu.make_async_copy(src_hbm.at[pl.ds(src_row, 1), :], dst_ref.at[pl.ds(dst_row, 1), :], sem)


def _dispatch_kernel(pad_start_ref, pad_cnt_ref, ra_ref, rb_ref, x_ref, xs_hbm, zbuf, sem, zsem):
    def fill_copy(e, k):
        return _row_copy(zbuf, 0, xs_hbm, pad_start_ref[e] + k, zsem)

    def tail_copy(k):
        start = pl.multiple_of(pad_start_ref[N_EXPERTS] + k * MOE_TM, MOE_TM)
        return pltpu.make_async_copy(zbuf, xs_hbm.at[pl.ds(start, MOE_TM), :], zsem)

    @pl.when(pl.program_id(0) == 0)
    def _():
        zbuf[...] = jnp.zeros_like(zbuf)
        for e in range(N_EXPERTS):
            lax.fori_loop(0, pad_cnt_ref[e], lambda k, c: (fill_copy(e, k).start(), c)[1], 0)
        lax.fori_loop(0, pad_cnt_ref[N_EXPERTS], lambda k, c: (tail_copy(k).start(), c)[1], 0)

    def issue(r, carry):
        _row_copy(x_ref, r, xs_hbm, ra_ref[0, 0, r], sem).start()
        _row_copy(x_ref, r, xs_hbm, rb_ref[0, 0, r], sem).start()
        return carry

    lax.fori_loop(0, GATHER_TM, issue, 0, unroll=ISSUE_UNROLL)

    for _ in range(TOP_K):
        pltpu.make_async_copy(x_ref, xs_hbm.at[pl.ds(0, GATHER_TM), :], sem).wait()

    @pl.when(pl.program_id(0) == 0)
    def _():
        for e in range(N_EXPERTS):
            lax.fori_loop(0, pad_cnt_ref[e], lambda k, c: (fill_copy(e, k).wait(), c)[1], 0)
        lax.fori_loop(0, pad_cnt_ref[N_EXPERTS], lambda k, c: (tail_copy(k).wait(), c)[1], 0)


def _dispatch_rows(pad_start, pad_cnt, ra, rb, x, rows_total):
    n = x.shape[0]
    smem = pl.BlockSpec((1, 1, GATHER_TM), lambda i, ps, pc: (i, 0, 0), memory_space=pltpu.SMEM)
    grid_spec = pltpu.PrefetchScalarGridSpec(
        num_scalar_prefetch=2,
        grid=(n // GATHER_TM,),
        in_specs=[smem, smem, pl.BlockSpec((GATHER_TM, D_MODEL), lambda i, ps, pc: (i, 0))],
        out_specs=pl.BlockSpec(memory_space=pl.ANY),
        scratch_shapes=[pltpu.VMEM((MOE_TM, D_MODEL), x.dtype), pltpu.SemaphoreType.DMA(()),
                        pltpu.SemaphoreType.DMA(())],
    )
    return pl.pallas_call(
        _dispatch_kernel,
        grid_spec=grid_spec,
        out_shape=jax.ShapeDtypeStruct((rows_total, D_MODEL), x.dtype),
        compiler_params=_cparams(("arbitrary",)),
        name="moe_dispatch",
    )(pad_start, pad_cnt, ra, rb, x)


def _moe_kernel(te_ref, act_ref, x_ref, w1_ref, w3_ref, w2_ref, o_ref):
    del te_ref
    i, f = pl.program_id(0), pl.program_id(1)

    @pl.when(act_ref[i] == 1)
    def _():
        xb = x_ref[...].astype(bf16)
        h = jax.nn.silu(_dot(xb, w1_ref[0])) * _dot(xb, w3_ref[0])
        part = _dot(h.astype(bf16), w2_ref[0])

        @pl.when(f == 0)
        def _():
            o_ref[...] = part

        @pl.when(f > 0)
        def _():
            o_ref[...] = o_ref[...] + part

    @pl.when((act_ref[i] == 0) & (f == 0))
    def _():
        o_ref[...] = jnp.zeros_like(o_ref)


def _moe_gmm(tile_expert, tile_active, xs, w1, w3, w2):
    n_tiles = tile_expert.shape[0]
    d_ff = w1.shape[2]
    grid_spec = pltpu.PrefetchScalarGridSpec(
        num_scalar_prefetch=2,
        grid=(n_tiles, d_ff // MOE_TF),
        in_specs=[pl.BlockSpec((MOE_TM, D_MODEL), lambda i, f, te, act: (i * act[i], 0)),
                  pl.BlockSpec((1, D_MODEL, MOE_TF), lambda i, f, te, act: (te[i], 0, f)),
                  pl.BlockSpec((1, D_MODEL, MOE_TF), lambda i, f, te, act: (te[i], 0, f)),
                  pl.BlockSpec((1, MOE_TF, D_MODEL), lambda i, f, te, act: (te[i], f, 0))],
        out_specs=pl.BlockSpec((MOE_TM, D_MODEL), lambda i, f, te, act: (i, 0)),
    )
    return pl.pallas_call(
        _moe_kernel,
        grid_spec=grid_spec,
        out_shape=jax.ShapeDtypeStruct((n_tiles * MOE_TM, D_MODEL), f32),
        compiler_params=_cparams(("arbitrary", "arbitrary")),
        name="moe_gmm",
    )(tile_expert, tile_active, xs, w1, w3, w2)


def _combine_kernel(ra_ref, rb_ref, ra_next, rb_next, ga_ref, gb_ref, x_ref, y_hbm, g_ref, b_ref, o_ref,
                    ybuf, sem):
    i = pl.program_id(0)
    slot = lax.rem(i, 2)

    def gather(ia_ref, ib_ref, s):
        def issue(r, carry):
            _row_copy(y_hbm, ia_ref[0, 0, r], ybuf.at[s, 0], r, sem.at[s, 0]).start()
            _row_copy(y_hbm, ib_ref[0, 0, r], ybuf.at[s, 1], r, sem.at[s, 1]).start()
            return carry

        lax.fori_loop(0, COMBINE_TM, issue, 0, unroll=ISSUE_UNROLL)

    @pl.when(i == 0)
    def _():
        gather(ra_ref, rb_ref, 0)

    @pl.when(i + 1 < pl.num_programs(0))
    def _():
        gather(ra_next, rb_next, 1 - slot)

    for k in range(TOP_K):
        pltpu.make_async_copy(y_hbm.at[pl.ds(0, COMBINE_TM), :], ybuf.at[slot, k], sem.at[slot, k]).wait()
    f = ga_ref[...] * ybuf[slot, 0] + gb_ref[...] * ybuf[slot, 1]
    o_ref[...] = _layer_norm_rows(DEEPNORM_ALPHA * x_ref[...] + f, g_ref[...], b_ref[...])


def _combine(ra, rb, ga, gb, x, ys, ln_g, ln_b):
    n = x.shape[0]
    last = n // COMBINE_TM - 1
    smem = pl.BlockSpec((1, 1, COMBINE_TM), lambda i: (i, 0, 0), memory_space=pltpu.SMEM)
    smem_next = pl.BlockSpec((1, 1, COMBINE_TM), lambda i: (jnp.minimum(i + 1, last), 0, 0),
                             memory_space=pltpu.SMEM)
    col = pl.BlockSpec((COMBINE_TM, 1), lambda i: (i, 0))
    row = pl.BlockSpec((COMBINE_TM, D_MODEL), lambda i: (i, 0))
    return pl.pallas_call(
        _combine_kernel,
        grid=(n // COMBINE_TM,),
        in_specs=[smem, smem, smem_next, smem_next, col, col, row, pl.BlockSpec(memory_space=pl.ANY),
                  _const_spec((1, D_MODEL)), _const_spec((1, D_MODEL))],
        out_specs=row,
        out_shape=jax.ShapeDtypeStruct((n, D_MODEL), f32),
        scratch_shapes=[pltpu.VMEM((2, TOP_K, COMBINE_TM, D_MODEL), f32), pltpu.SemaphoreType.DMA((2, TOP_K))],
        compiler_params=_cparams(("arbitrary",)),
        name="moe_combine",
    )(ra, rb, ra, rb, ga, gb, x, ys, ln_g, ln_b)


def _moe_plan(sel, gates):
    n = sel.shape[0]
    rows_total = TOP_K * n + N_EXPERTS * MOE_TM
    n_tiles = rows_total // MOE_TM
    seli = sel.astype(jnp.int32)
    cnt = jnp.sum(seli, axis=0)
    pos = jnp.cumsum(seli, axis=0) - seli
    padded = ((cnt + MOE_TM - 1) // MOE_TM) * MOE_TM
    bounds = jnp.cumsum(padded)
    row = (bounds - padded)[None, :] + pos
    pad_start = jnp.concatenate([bounds - padded + cnt, bounds[-1:]]).astype(jnp.int32)
    pad_cnt = jnp.concatenate([padded - cnt, (rows_total - bounds[-1:]) // MOE_TM]).astype(jnp.int32)
    tile_start = jnp.arange(n_tiles, dtype=jnp.int32) * MOE_TM
    last_used = jnp.max(jnp.where(cnt > 0, jnp.arange(N_EXPERTS, dtype=jnp.int32), 0))
    tile_expert = jnp.sum((tile_start[:, None] >= bounds[None, :]).astype(jnp.int32), axis=1)
    tile_expert = jnp.minimum(tile_expert, last_used)
    tile_active = (tile_start < bounds[-1]).astype(jnp.int32)
    eid = jnp.arange(N_EXPERTS, dtype=jnp.int32)[None, :]
    is_a = eid == jnp.min(jnp.where(sel, eid, N_EXPERTS), axis=1, keepdims=True)
    is_b = eid == jnp.max(jnp.where(sel, eid, -1), axis=1, keepdims=True)
    pick = lambda a, m: jnp.sum(jnp.where(m, a, 0), axis=1)
    return (pad_start, pad_cnt, tile_expert, tile_active,
            pick(row, is_a).astype(jnp.int32), pick(row, is_b).astype(jnp.int32),
            pick(gates, is_a)[:, None], pick(gates, is_b)[:, None])


def kernel(x, s5_w_in, s5_a_re, s5_a_im, s5_log_dt, s5_b_re, s5_b_im, s5_c_re, s5_c_im, s5_d, s5_w_glu,
           attn_w_qkv, attn_w_o, rel_bias, ffn_w1, ffn_w3, ffn_w2, moe_w_router, moe_w1, moe_w3, moe_w2,
           ln_g, ln_b):
    batch, seq, d = x.shape
    n = batch * seq
    assert d == D_MODEL
    assert seq % (DILATED_PAIRS[-1][1] * ATTN_BLOCK * ATT_QB) == 0
    assert seq % S5_TB == 0 and seq % FFN_TM == 0 and FFN_TM == MRG_TM
    vec = lambda v: v.reshape(1, D_MODEL).astype(f32)

    bre, bim, cre, cimn, tab = _s5_tables(s5_a_re[0], s5_a_im[0], s5_log_dt[0], s5_b_re[0], s5_b_im[0],
                                          s5_c_re[0], s5_c_im[0])
    x1 = _s5_layer(x.reshape(n, d), s5_w_in[0].astype(bf16), bre, bim, cre, cimn, tab, vec(s5_d[0]),
                   s5_w_glu[0].astype(bf16), vec(ln_g[0, 0]), vec(ln_b[0, 0]), batch)
    x2, *xgs = _ffn_layer(x1, ffn_w1[0].astype(bf16), ffn_w3[0].astype(bf16), ffn_w2[0].astype(bf16),
                          vec(ln_g[0, 1]), vec(ln_b[0, 1]), batch)

    wq = attn_w_qkv[0].reshape(d, N_DIL_GROUPS, 3 * d).astype(bf16)
    bias = _attn_bias(rel_bias)
    os_, lses = [], []
    for g, (_, dil) in enumerate(DILATED_PAIRS):
        qkv = _qkv_proj(xgs[g].reshape(n, d), wq[:, g])
        o, lse = _attention(qkv, bias[g], seq // (dil * ATTN_BLOCK))
        os_.append(o if g == 0 else o.reshape(batch, dil, seq // dil, d))
        lses.append(lse if g == 0 else lse.reshape(batch, dil, seq // dil, LANE))
    expand = (jnp.arange(LANE)[:, None] == (jnp.arange(D_MODEL)[None, :] // HEAD_DIM)).astype(bf16)
    wr = jnp.pad(moe_w_router[0].astype(f32), ((0, 0), (0, LANE - N_EXPERTS)))
    wr_hi, wr_lo = _split_bf16(wr)
    x3, gates, sel = _merge_layer(os_, lses, x2, expand, attn_w_o[0].astype(bf16), vec(ln_g[1, 0]),
                                  vec(ln_b[1, 0]), wr_hi, wr_lo, batch)

    pad_start, pad_cnt, tile_expert, tile_active, ra, rb, ga, gb = _moe_plan(sel[:, :N_EXPERTS] > 0.5,
                                                                             gates[:, :N_EXPERTS])
    xs = _dispatch_rows(pad_start, pad_cnt, ra.reshape(-1, 1, GATHER_TM), rb.reshape(-1, 1, GATHER_TM), x3,
                        TOP_K * n + N_EXPERTS * MOE_TM)
    ys = _moe_gmm(tile_expert, tile_active, xs, moe_w1[0].astype(bf16), moe_w3[0].astype(bf16),
                  moe_w2[0].astype(bf16))
    out = _combine(ra.reshape(-1, 1, COMBINE_TM), rb.reshape(-1, 1, COMBINE_TM), ga, gb, x3, ys,
                   vec(ln_g[1, 1]), vec(ln_b[1, 1]))
    return out.reshape(batch, seq, d)
```

```python
import functools
import math

import numpy as np
import jax
import jax.numpy as jnp
from jax import lax
from jax.experimental import pallas as pl
from jax.experimental.pallas import tpu as pltpu

D_MODEL = 1024
S5_GROUP = 16
S5_GROUPS = D_MODEL // S5_GROUP
S5_STATE = 64
S5_WIDTH = S5_GROUPS * S5_STATE
HEAD_DIM = 64
N_HEADS = D_MODEL // HEAD_DIM
DILATED_PAIRS = ((128, 1), (512, 4), (2048, 16))
N_DIL_GROUPS = len(DILATED_PAIRS)
ATTN_BLOCK = 128
N_BUCKETS = 32
MAX_DISTANCE = 2048
N_EXPERTS = 8
TOP_K = 2
DEPTH = 2
DEEPNORM_ALPHA = (2 * DEPTH) ** 0.25
LN_EPS = 1e-5
NEG_INF = -1e30
LOG2_E = math.log2(math.e)

LANE = 128
SUBLANE = 8
BF16_SUBLANES = 16
MXU_DIM = 256
VMEM_LIMIT = 56 * 1024 * 1024

S5_T = 32
S5_TB = SUBLANE * S5_T
S5_LC = 512
S5_KT = MXU_DIM
S5_NT = D_MODEL // S5_KT
S5_ST = S5_WIDTH // S5_NT
FFN_TM = 512
QKV_TM = 1024
ATT_QB = 4
ATT_ROWS = 32
MRG_TM = 512
MOE_TM = 512
MOE_TF = 1792
GATHER_TM = 512
COMBINE_TM = 256
ISSUE_UNROLL = 8

f32 = jnp.float32
bf16 = jnp.bfloat16


def _cparams(sem):
    return pltpu.CompilerParams(dimension_semantics=sem, vmem_limit_bytes=VMEM_LIMIT)


def _const_spec(shape):
    nd = len(shape)
    return pl.BlockSpec(shape, lambda *_: (0,) * nd, pipeline_mode=pl.Buffered(1))


def _layer_norm_rows(v, g, b):
    mu = jnp.mean(v, axis=-1, keepdims=True)
    c = v - mu
    var = jnp.mean(c * c, axis=-1, keepdims=True)
    return c * lax.rsqrt(var + LN_EPS) * g + b


def _gelu_tanh(x):
    return 0.5 * x * (1.0 + jnp.tanh(math.sqrt(2.0 / math.pi) * (x + 0.044715 * (x * x * x))))


def _dot(a, b):
    return jnp.dot(a, b, preferred_element_type=f32)


_TAB_A = 0
_TAB_P1 = 2
_TAB_P2 = 4
_TAB_P4 = 6
_TAB_PC_RE = 8
_TAB_PC_IM = 16
_TAB_ROWS = 24


def _split3_bf16(v):
    hi = v.astype(bf16)
    r1 = v - hi.astype(f32)
    mid = r1.astype(bf16)
    return hi, mid, (r1 - mid.astype(f32)).astype(bf16)


def _permute_rows_f32(perm, v):
    hi, mid, lo = _split3_bf16(v)
    return _dot(perm, hi) + _dot(perm, mid) + _dot(perm, lo)


def _s5_kernel(x_ref, pf_ref, pb_ref, win_ref, bre_ref, bim_ref, cre_ref, cimn_ref, tab_ref, d_ref, wglu_ref,
               g_ref, b_ref, o_ref, u_ref, hre_ref, him_ref, car_re, car_im):
    @pl.when(pl.program_id(0) == 0)
    def _():
        car_re[...] = jnp.zeros_like(car_re)
        car_im[...] = jnp.zeros_like(car_im)

    nb = x_ref.shape[0]
    phases = [_s5_tile(x_ref[b], functools.partial(o_ref.__setitem__, b), pf_ref, pb_ref, win_ref, bre_ref,
                       bim_ref, cre_ref, cimn_ref, tab_ref, d_ref, wglu_ref, g_ref, b_ref, u_ref.at[b],
                       hre_ref.at[b], him_ref.at[b], car_re.at[b], car_im.at[b]) for b in range(nb)]
    _interleave(phases[0][0], [])
    for b in range(nb):
        mxu_work = (phases[b - 1][2] if b > 0 else []) + (phases[b + 1][0] if b + 1 < nb else [])
        _interleave(phases[b][1], mxu_work)
    _interleave(phases[nb - 1][2], [])


def _interleave(a, b):
    done = 0
    for i, piece in enumerate(a):
        piece()
        while done < (i + 1) * len(b) // len(a):
            b[done]()
            done += 1


def _s5_tile(x, emit, pf_ref, pb_ref, win_ref, bre_ref, bim_ref, cre_ref, cimn_ref, tab_ref, d_ref, wglu_ref,
             g_ref, b_ref, u_ref, hre_ref, him_ref, car_re, car_im):
    T = S5_T

    def project_u():
        xs = _dot(pf_ref[...], x.astype(bf16)).astype(bf16)
        u_ref[...] = _dot(xs, win_ref[...])

    def project_b(kt):
        ub = u_ref[:, kt * S5_KT:(kt + 1) * S5_KT].astype(bf16)
        hre_ref[:, kt * S5_ST:(kt + 1) * S5_ST] = _dot(ub, bre_ref[kt])
        him_ref[:, kt * S5_ST:(kt + 1) * S5_ST] = _dot(ub, bim_ref[kt])

    def bcast(r, sl):
        return jnp.broadcast_to(tab_ref[r:r + 1, sl], (SUBLANE, S5_LC))

    def cmul(ar, ai, xr, xi):
        return ar * xr - ai * xi, ar * xi + ai * xr

    def recur(lc):
        row = lax.broadcasted_iota(jnp.int32, (SUBLANE, S5_LC), 0)
        sl = slice(lc * S5_LC, (lc + 1) * S5_LC)
        ar, ai = bcast(_TAB_A, sl), bcast(_TAB_A + 1, sl)

        er, ei = hre_ref[0:SUBLANE, sl], him_ref[0:SUBLANE, sl]
        for t in range(1, T):
            rows = slice(t * SUBLANE, (t + 1) * SUBLANE)
            pr, pi = cmul(ar, ai, er, ei)
            er = pr + hre_ref[rows, sl]
            ei = pi + him_ref[rows, sl]
            hre_ref[rows, sl] = er
            him_ref[rows, sl] = ei

        xr, xi = er, ei
        for s, tr in ((1, _TAB_P1), (2, _TAB_P2), (4, _TAB_P4)):
            sr = jnp.where(row >= s, pltpu.roll(xr, s, 0), 0.0)
            si = jnp.where(row >= s, pltpu.roll(xi, s, 0), 0.0)
            mr, mi = cmul(bcast(tr, sl), bcast(tr + 1, sl), sr, si)
            xr, xi = xr + mr, xi + mi
        shr = jnp.where(row >= 1, pltpu.roll(xr, 1, 0), 0.0)
        shi = jnp.where(row >= 1, pltpu.roll(xi, 1, 0), 0.0)
        cr, ci = car_re[:, sl], car_im[:, sl]
        pcr = tab_ref[_TAB_PC_RE:_TAB_PC_RE + SUBLANE, sl]
        pci = tab_ref[_TAB_PC_IM:_TAB_PC_IM + SUBLANE, sl]
        hr0, hi0 = cmul(pcr, pci, cr, ci)
        hr0, hi0 = hr0 + shr, hi0 + shi
        nr, ni = cmul(bcast(_TAB_P1, sl), bcast(_TAB_P1 + 1, sl), hr0, hi0)
        nr, ni = nr + er, ni + ei
        car_re[:, sl] = jnp.broadcast_to(nr[SUBLANE - 1:SUBLANE], (SUBLANE, S5_LC))
        car_im[:, sl] = jnp.broadcast_to(ni[SUBLANE - 1:SUBLANE], (SUBLANE, S5_LC))

        zr, zi = hr0, hi0
        for t in range(T):
            rows = slice(t * SUBLANE, (t + 1) * SUBLANE)
            zr, zi = cmul(ar, ai, zr, zi)
            hre_ref[rows, sl] = hre_ref[rows, sl] + zr
            him_ref[rows, sl] = him_ref[rows, sl] + zi

    gs, hs = [], []

    def project_c(nt):
        hr = hre_ref[:, nt * S5_ST:(nt + 1) * S5_ST].astype(bf16)
        hi = him_ref[:, nt * S5_ST:(nt + 1) * S5_ST].astype(bf16)
        csl = slice(nt * S5_KT, (nt + 1) * S5_KT)
        y = _dot(hr, cre_ref[nt]) + _dot(hi, cimn_ref[nt]) + d_ref[:, csl] * u_ref[:, csl]
        gs.append(_gelu_tanh(y).astype(bf16))

    def glu(j):
        g = jnp.concatenate(gs, axis=1)
        val = _dot(g, wglu_ref[:, j * S5_KT:(j + 1) * S5_KT])
        gate = _dot(g, wglu_ref[:, D_MODEL + j * S5_KT:D_MODEL + (j + 1) * S5_KT])
        hs.append(val * jax.nn.sigmoid(gate))

    def finish():
        h = _permute_rows_f32(pb_ref[...], jnp.concatenate(hs, axis=1))
        emit(_layer_norm_rows(DEEPNORM_ALPHA * x + h, g_ref[...], b_ref[...]))

    part = functools.partial
    return ([project_u] + [part(project_b, kt) for kt in range(S5_NT)],
            [part(recur, lc) for lc in range(S5_WIDTH // S5_LC)],
            [part(project_c, nt) for nt in range(S5_NT)] + [part(glu, j) for j in range(S5_NT)] + [finish])


def _s5_tables(a_re, a_im, log_dt, b_re, b_im, c_re, c_im):
    dt = jnp.exp(log_dt.astype(f32))[:, None]
    lam_re = jnp.minimum(a_re.astype(f32), -1e-4)
    lam_im = a_im.astype(f32)
    mag = jnp.exp(lam_re * dt)
    ab_re = mag * jnp.cos(lam_im * dt)
    ab_im = mag * jnp.sin(lam_im * dt)
    den = lam_re * lam_re + lam_im * lam_im
    nr = ab_re - 1.0
    q_re = (nr * lam_re + ab_im * lam_im) / den
    q_im = (ab_im * lam_re - nr * lam_im) / den
    br, bi = b_re.astype(f32), b_im.astype(f32)
    bb_re = q_re[..., None] * br - q_im[..., None] * bi
    bb_im = q_re[..., None] * bi + q_im[..., None] * br

    gpt = S5_KT // S5_GROUP
    eye = jnp.eye(gpt, dtype=f32)

    def b_blocks(bb):
        t = bb.reshape(S5_NT, gpt, S5_STATE, S5_GROUP)
        blk = jnp.einsum('ngpc,gh->ngchp', t, eye)
        return blk.reshape(S5_NT, S5_KT, S5_ST).astype(bf16)

    def c_blocks(cc):
        t = cc.reshape(S5_NT, gpt, S5_GROUP, S5_STATE)
        blk = jnp.einsum('ngcp,gh->ngphc', t, eye)
        return blk.reshape(S5_NT, S5_ST, S5_KT).astype(bf16)

    def csq(z):
        return z[0] * z[0] - z[1] * z[1], 2.0 * z[0] * z[1]

    def cmul(x, y):
        return x[0] * y[0] - x[1] * y[1], x[0] * y[1] + x[1] * y[0]

    a1 = (ab_re.reshape(-1), ab_im.reshape(-1))
    p = a1
    for _ in range(int(math.log2(S5_T))):
        p = csq(p)
    p1 = p
    p2 = csq(p1)
    p4 = csq(p2)
    pc = [(jnp.ones_like(p1[0]), jnp.zeros_like(p1[0]))]
    for _ in range(SUBLANE - 1):
        pc.append(cmul(pc[-1], p1))
    rows = [a1[0], a1[1], p1[0], p1[1], p2[0], p2[1], p4[0], p4[1]]
    rows += [z[0] for z in pc] + [z[1] for z in pc]
    tab = jnp.stack(rows, axis=0)
    return (b_blocks(bb_re), b_blocks(bb_im), c_blocks(c_re.astype(f32)), c_blocks(-c_im.astype(f32)), tab)


def _scan_perm():
    p = np.zeros((S5_TB, S5_TB), np.float32)
    for c in range(SUBLANE):
        for t in range(S5_T):
            p[t * SUBLANE + c, c * S5_T + t] = 1.0
    return p


def _perm_rows(dilation):
    return max(LANE, BF16_SUBLANES * dilation)


def _stride_perm(tile, dilation):
    p = np.zeros((tile, tile), np.float32)
    per = tile // dilation
    for j in range(per):
        for r in range(dilation):
            p[r * per + j, j * dilation + r] = 1.0
    return p


def _s5_layer(x, w_in, bre, bim, cre, cimn, tab, d_skip, w_glu, ln_g, ln_b, batch):
    n = x.shape[0]
    seq = n // batch
    row_spec = pl.BlockSpec((batch, S5_TB, D_MODEL), lambda t: (0, t, 0))
    perm = _scan_perm()
    out = pl.pallas_call(
        _s5_kernel,
        grid=(seq // S5_TB,),
        in_specs=[row_spec,
                  _const_spec((S5_TB, S5_TB)), _const_spec((S5_TB, S5_TB)),
                  _const_spec((D_MODEL, D_MODEL)),
                  _const_spec((S5_NT, S5_KT, S5_ST)), _const_spec((S5_NT, S5_KT, S5_ST)),
                  _const_spec((S5_NT, S5_ST, S5_KT)), _const_spec((S5_NT, S5_ST, S5_KT)),
                  _const_spec((_TAB_ROWS, S5_WIDTH)),
                  _const_spec((1, D_MODEL)),
                  _const_spec((D_MODEL, 2 * D_MODEL)),
                  _const_spec((1, D_MODEL)), _const_spec((1, D_MODEL))],
        out_specs=row_spec,
        out_shape=jax.ShapeDtypeStruct((batch, seq, D_MODEL), f32),
        scratch_shapes=[pltpu.VMEM((batch, S5_TB, D_MODEL), f32),
                        pltpu.VMEM((batch, S5_TB, S5_WIDTH), f32), pltpu.VMEM((batch, S5_TB, S5_WIDTH), f32),
                        pltpu.VMEM((batch, SUBLANE, S5_WIDTH), f32),
                        pltpu.VMEM((batch, SUBLANE, S5_WIDTH), f32)],
        compiler_params=_cparams(("arbitrary",)),
        name="s5_mixer",
    )(x.reshape(batch, seq, D_MODEL), jnp.asarray(perm, bf16), jnp.asarray(perm.T, bf16), w_in, bre, bim, cre,
      cimn, tab, d_skip, w_glu, ln_g, ln_b)
    return out.reshape(n, D_MODEL)


def _ffn_kernel(x_ref, w1_ref, w3_ref, w2_ref, g_ref, b_ref, *rest):
    perm_refs = rest[:N_DIL_GROUPS - 1]
    o_ref = rest[N_DIL_GROUPS - 1]
    xg_refs = rest[N_DIL_GROUPS:]
    x = x_ref[...]
    xb = x.astype(bf16)
    h = jax.nn.silu(_dot(xb, w1_ref[...])) * _dot(xb, w3_ref[...])
    f = _dot(h.astype(bf16), w2_ref[...])
    y = _layer_norm_rows(DEEPNORM_ALPHA * x + f, g_ref[...], b_ref[...])
    o_ref[...] = y
    yb = y.astype(bf16)
    xg_refs[0][...] = yb
    for perm_ref, xg_ref in zip(perm_refs, xg_refs[1:]):
        dil = xg_ref.shape[1]
        blk = perm_ref.shape[0]
        piece = blk // dil
        for s in range(FFN_TM // blk):
            v = _dot(perm_ref[...], yb[s * blk:(s + 1) * blk]).astype(bf16)
            xg_ref[0, :, s * piece:(s + 1) * piece, :] = v.reshape(dil, piece, D_MODEL)


def _strided_spec(tile, dilation, width, tiles_per_seq):
    return pl.BlockSpec((1, dilation, tile // dilation, width),
                        lambda i: (i // tiles_per_seq, 0, lax.rem(i, tiles_per_seq), 0))


def _ffn_layer(x, w1, w3, w2, ln_g, ln_b, batch):
    n = x.shape[0]
    seq = n // batch
    d_ff = w1.shape[1]
    row_spec = pl.BlockSpec((FFN_TM, D_MODEL), lambda i: (i, 0))
    dils = [dil for _, dil in DILATED_PAIRS[1:]]
    perms = [jnp.asarray(_stride_perm(_perm_rows(dil), dil), bf16) for dil in dils]
    return pl.pallas_call(
        _ffn_kernel,
        grid=(n // FFN_TM,),
        in_specs=[row_spec, _const_spec((D_MODEL, d_ff)), _const_spec((D_MODEL, d_ff)),
                  _const_spec((d_ff, D_MODEL)), _const_spec((1, D_MODEL)), _const_spec((1, D_MODEL))]
                 + [_const_spec(p.shape) for p in perms],
        out_specs=[row_spec, row_spec] + [_strided_spec(FFN_TM, dil, D_MODEL, seq // FFN_TM) for dil in dils],
        out_shape=[jax.ShapeDtypeStruct((n, D_MODEL), f32), jax.ShapeDtypeStruct((n, D_MODEL), bf16)]
                  + [jax.ShapeDtypeStruct((batch, dil, seq // dil, D_MODEL), bf16) for dil in dils],
        compiler_params=_cparams(("arbitrary",)),
        name="dense_ffn",
    )(x, w1, w3, w2, ln_g, ln_b, *perms)


def _qkv_kernel(x_ref, w_ref, o_ref):
    x = x_ref[...]
    for j in range(3):
        cs = slice(j * D_MODEL, (j + 1) * D_MODEL)
        r = _dot(x, w_ref[:, cs])
        if j == 0:
            r = r * (LOG2_E / math.sqrt(HEAD_DIM))
        o_ref[:, cs] = r.astype(o_ref.dtype)


def _qkv_proj(xg, wg):
    n = xg.shape[0]
    return pl.pallas_call(
        _qkv_kernel,
        grid=(n // QKV_TM,),
        in_specs=[pl.BlockSpec((QKV_TM, D_MODEL), lambda i: (i, 0)), _const_spec((D_MODEL, 3 * D_MODEL))],
        out_specs=pl.BlockSpec((QKV_TM, 3 * D_MODEL), lambda i: (i, 0)),
        out_shape=jax.ShapeDtypeStruct((n, 3 * D_MODEL), bf16),
        compiler_params=_cparams(("arbitrary",)),
        name="qkv_proj",
    )(xg, wg)


def _t5_bucket(dist):
    max_exact = N_BUCKETS // 2
    d = np.maximum(dist, 0)
    large = max_exact + (np.log(np.maximum(d, 1) / max_exact) / math.log(MAX_DISTANCE / max_exact)
                         * (N_BUCKETS - max_exact)).astype(np.int64)
    large = np.minimum(large, N_BUCKETS - 1)
    return np.where(d < max_exact, d, large).astype(np.int32)


def _attn_bias(rel_bias):
    L = ATTN_BLOCK
    qi = np.arange(L)[:, None]
    ki = np.arange(2 * L)[None, :]
    off = qi + L - ki
    out = []
    for window, dilation in DILATED_PAIRS:
        band = (off >= 0) & (off <= window // dilation)
        bucket = _t5_bucket(np.clip(off, 0, None) * dilation)
        onehot = jnp.asarray(bucket[None] == np.arange(N_BUCKETS)[:, None, None], f32)
        bias = jnp.einsum('blk,bh->hlk', onehot, rel_bias.astype(f32), precision=lax.Precision.HIGHEST)
        out.append(jnp.where(band[None], bias * LOG2_E, NEG_INF))
    return jnp.stack(out, axis=0)


def _attn_kernel(q_ref, kp_ref, kc_ref, vp_ref, vc_ref, bias_ref, o_ref, lse_ref, *, blocks_per_seq):
    L = ATTN_BLOCK
    low = lax.broadcasted_iota(jnp.int32, (L, LANE), 1) < HEAD_DIM
    lane_slab = lax.broadcasted_iota(jnp.int32, (ATT_ROWS, LANE), 1)
    is_first = lax.rem(pl.program_id(0) * ATT_QB, blocks_per_seq) == 0
    col = lax.broadcasted_iota(jnp.int32, (ATT_ROWS, 2 * L), 1)
    pen = jnp.where((col < L) & is_first, NEG_INF, 0.0).astype(f32)
    for s in range(ATT_QB):
        rs = slice(s * L, (s + 1) * L)
        ks = slice((s - 1) * L, (s + 1) * L)
        for hp in range(N_HEADS // 2):
            cs = slice(hp * LANE, (hp + 1) * LANE)
            qp = q_ref[rs, cs]
            if s == 0:
                kpair = jnp.concatenate([kp_ref[:, cs], kc_ref[rs, cs]], axis=0)
                vpair = jnp.concatenate([vp_ref[:, cs], vc_ref[rs, cs]], axis=0)
            else:
                kpair, vpair = kc_ref[ks, cs], vc_ref[ks, cs]
            outs = []
            for half in range(2):
                h = 2 * hp + half
                qm = jnp.where(low if half == 0 else jnp.logical_not(low), qp, jnp.zeros_like(qp))
                logits = lax.dot_general(qm, kpair, (((1,), (1,)), ((), ())),
                                         preferred_element_type=f32)
                ps = []
                for r0 in range(0, L, ATT_ROWS):
                    rows = slice(s * L + r0, s * L + r0 + ATT_ROWS)
                    slab = logits[r0:r0 + ATT_ROWS] + bias_ref[h, r0:r0 + ATT_ROWS, :]
                    if s == 0:
                        slab = slab + pen
                    m = jnp.max(slab, axis=-1, keepdims=True)
                    p = jnp.exp2(slab - m)
                    ssum = jnp.sum(p, axis=-1, keepdims=True)
                    ps.append((p * (1.0 / ssum)).astype(bf16))
                    lse = (m + jnp.log2(ssum)) * math.log(2.0)
                    lse_ref[rows, :] = jnp.where(lane_slab == h, lse,
                                                 lse_ref[rows, :] if h else jnp.zeros((ATT_ROWS, LANE), f32))
                outs.append(_dot(jnp.concatenate(ps, axis=0), vpair))
            o_ref[rs, cs] = jnp.where(low, outs[0], outs[1]).astype(o_ref.dtype)


def _attention(qkv, bias, blocks_per_seq):
    n = qkv.shape[0]
    L = ATTN_BLOCK
    tq = ATT_QB * L
    assert blocks_per_seq % ATT_QB == 0

    def cur(col):
        return pl.BlockSpec((tq, D_MODEL), lambda i: (i, col))

    def prev(col):
        return pl.BlockSpec((L, D_MODEL), lambda i: (jnp.maximum(i * ATT_QB - 1, 0), col))

    return pl.pallas_call(
        functools.partial(_attn_kernel, blocks_per_seq=blocks_per_seq),
        grid=(n // tq,),
        in_specs=[cur(0), prev(1), cur(1), prev(2), cur(2), _const_spec((N_HEADS, L, 2 * L))],
        out_specs=[pl.BlockSpec((tq, D_MODEL), lambda i: (i, 0)), pl.BlockSpec((tq, LANE), lambda i: (i, 0))],
        out_shape=[jax.ShapeDtypeStruct((n, D_MODEL), bf16), jax.ShapeDtypeStruct((n, LANE), f32)],
        compiler_params=_cparams(("arbitrary",)),
        name="dilated_attn",
    )(qkv, qkv, qkv, qkv, qkv, bias)


def _split_bf16(v):
    hi = v.astype(bf16)
    return hi, (v - hi.astype(f32)).astype(bf16)


def _unstride_rows(ref, perm_ref, apply):
    dil, per, width = ref.shape[1:]
    blk = perm_ref.shape[0]
    piece = blk // dil
    outs = []
    for s in range(dil * per // blk):
        v = ref[0, :, s * piece:(s + 1) * piece, :].reshape(blk, width)
        outs.append(apply(perm_ref[...], v))
    return jnp.concatenate(outs, axis=0)


def _merge_kernel(*refs):
    ng = N_DIL_GROUPS
    o_refs, lse_refs, perm_refs = refs[:ng], refs[ng:2 * ng], refs[2 * ng:3 * ng - 1]
    (x_ref, e_ref, wo_ref, g_ref, b_ref, wrh_ref, wrl_ref, x3_ref, gate_ref, sel_ref) = refs[3 * ng - 1:]
    os_ = [o_refs[0][...].astype(f32)]
    ls = [lse_refs[0][...]]
    for o_ref, lse_ref, perm_ref in zip(o_refs[1:], lse_refs[1:], perm_refs):
        os_.append(_unstride_rows(o_ref, perm_ref, lambda p, v: _dot(p, v)))
        ls.append(_unstride_rows(lse_ref, perm_ref, _permute_rows_f32))
    mx = functools.reduce(jnp.maximum, ls)
    es = [jnp.exp(l - mx) for l in ls]
    den = functools.reduce(jnp.add, es)
    merged = None
    for g in range(ng):
        wfull = _dot((es[g] / den).astype(bf16), e_ref[...])
        term = wfull * os_[g]
        merged = term if merged is None else merged + term
    att = _dot(merged.astype(bf16), wo_ref[...])
    x3 = _layer_norm_rows(DEEPNORM_ALPHA * x_ref[...] + att, g_ref[...], b_ref[...])
    x3_ref[...] = x3

    xh, xl = _split_bf16(x3)
    logits = _dot(xh, wrh_ref[...]) + _dot(xl, wrh_ref[...]) + _dot(xh, wrl_ref[...])
    lane = lax.broadcasted_iota(jnp.int32, logits.shape, 1)
    neg = -jnp.inf
    logits = jnp.where(lane < N_EXPERTS, logits, neg)
    m1 = jnp.max(logits, axis=-1, keepdims=True)
    i1 = jnp.min(jnp.where(logits == m1, lane, LANE), axis=-1, keepdims=True)
    pick1 = lane == i1
    rest = jnp.where(pick1, neg, logits)
    m2 = jnp.max(rest, axis=-1, keepdims=True)
    i2 = jnp.min(jnp.where(rest == m2, lane, LANE), axis=-1, keepdims=True)
    pick2 = lane == i2
    e2 = jnp.exp(m2 - m1)
    gate_ref[...] = jnp.where(pick1, 1.0 / (1.0 + e2), jnp.where(pick2, e2 / (1.0 + e2), 0.0))
    sel_ref[...] = jnp.where(pick1 | pick2, 1.0, 0.0)


def _merge_layer(os_, lses, x, expand, w_o, ln_g, ln_b, wr_hi, wr_lo, batch):
    n = x.shape[0]
    tiles_per_seq = n // batch // MRG_TM
    row = lambda w: pl.BlockSpec((MRG_TM, w), lambda i: (i, 0))
    dils = [dil for _, dil in DILATED_PAIRS[1:]]
    perms = [jnp.asarray(_stride_perm(_perm_rows(dil), dil).T, bf16) for dil in dils]
    return pl.pallas_call(
        _merge_kernel,
        grid=(n // MRG_TM,),
        in_specs=[row(D_MODEL)] + [_strided_spec(MRG_TM, dil, D_MODEL, tiles_per_seq) for dil in dils]
                 + [row(LANE)] + [_strided_spec(MRG_TM, dil, LANE, tiles_per_seq) for dil in dils]
                 + [_const_spec(p.shape) for p in perms]
                 + [row(D_MODEL), _const_spec((LANE, D_MODEL)),
                    _const_spec((D_MODEL, D_MODEL)), _const_spec((1, D_MODEL)), _const_spec((1, D_MODEL)),
                    _const_spec((D_MODEL, LANE)), _const_spec((D_MODEL, LANE))],
        out_specs=[row(D_MODEL), row(LANE), row(LANE)],
        out_shape=[jax.ShapeDtypeStruct((n, D_MODEL), f32), jax.ShapeDtypeStruct((n, LANE), f32),
                   jax.ShapeDtypeStruct((n, LANE), f32)],
        compiler_params=_cparams(("arbitrary",)),
        name="attn_merge",
    )(*os_, *lses, *perms, x, expand, w_o, ln_g, ln_b, wr_hi, wr_lo)


def _row_copy(src_hbm, src_row, dst_ref, dst_row, sem):
    return pltpu.make_async_copy(src_hbm.at[pl.ds(src_row, 1), :], dst_ref.at[pl.ds(dst_row, 1), :], sem)


def _dispatch_kernel(pad_start_ref, pad_cnt_ref, ra_ref, rb_ref, x_ref, xs_hbm, zbuf, sem, zsem):
    def fill_copy(e, k):
        return _row_copy(zbuf, 0, xs_hbm, pad_start_ref[e] + k, zsem)

    def tail_copy(k):
        start = pl.multiple_of(pad_start_ref[N_EXPERTS] + k * MOE_TM, MOE_TM)
        return pltpu.make_async_copy(zbuf, xs_hbm.at[pl.ds(start, MOE_TM), :], zsem)

    @pl.when(pl.program_id(0) == 0)
    def _():
        zbuf[...] = jnp.zeros_like(zbuf)
        for e in range(N_EXPERTS):
            lax.fori_loop(0, pad_cnt_ref[e], lambda k, c: (fill_copy(e, k).start(), c)[1], 0)
        lax.fori_loop(0, pad_cnt_ref[N_EXPERTS], lambda k, c: (tail_copy(k).start(), c)[1], 0)

    def issue(r, carry):
        _row_copy(x_ref, r, xs_hbm, ra_ref[0, 0, r], sem).start()
        _row_copy(x_ref, r, xs_hbm, rb_ref[0, 0, r], sem).start()
        return carry

    lax.fori_loop(0, GATHER_TM, issue, 0, unroll=ISSUE_UNROLL)

    for _ in range(TOP_K):
        pltpu.make_async_copy(x_ref, xs_hbm.at[pl.ds(0, GATHER_TM), :], sem).wait()

    @pl.when(pl.program_id(0) == 0)
    def _():
        for e in range(N_EXPERTS):
            lax.fori_loop(0, pad_cnt_ref[e], lambda k, c: (fill_copy(e, k).wait(), c)[1], 0)
        lax.fori_loop(0, pad_cnt_ref[N_EXPERTS], lambda k, c: (tail_copy(k).wait(), c)[1], 0)


def _dispatch_rows(pad_start, pad_cnt, ra, rb, x, rows_total):
    n = x.shape[0]
    smem = pl.BlockSpec((1, 1, GATHER_TM), lambda i, ps, pc: (i, 0, 0), memory_space=pltpu.SMEM)
    grid_spec = pltpu.PrefetchScalarGridSpec(
        num_scalar_prefetch=2,
        grid=(n // GATHER_TM,),
        in_specs=[smem, smem, pl.BlockSpec((GATHER_TM, D_MODEL), lambda i, ps, pc: (i, 0))],
        out_specs=pl.BlockSpec(memory_space=pl.ANY),
        scratch_shapes=[pltpu.VMEM((MOE_TM, D_MODEL), x.dtype), pltpu.SemaphoreType.DMA(()),
                        pltpu.SemaphoreType.DMA(())],
    )
    return pl.pallas_call(
        _dispatch_kernel,
        grid_spec=grid_spec,
        out_shape=jax.ShapeDtypeStruct((rows_total, D_MODEL), x.dtype),
        compiler_params=_cparams(("arbitrary",)),
        name="moe_dispatch",
    )(pad_start, pad_cnt, ra, rb, x)


def _moe_kernel(te_ref, act_ref, x_ref, w1_ref, w3_ref, w2_ref, o_ref):
    del te_ref
    i, f = pl.program_id(0), pl.program_id(1)

    @pl.when(act_ref[i] == 1)
    def _():
        xb = x_ref[...].astype(bf16)
        h = jax.nn.silu(_dot(xb, w1_ref[0])) * _dot(xb, w3_ref[0])
        part = _dot(h.astype(bf16), w2_ref[0])

        @pl.when(f == 0)
        def _():
            o_ref[...] = part

        @pl.when(f > 0)
        def _():
            o_ref[...] = o_ref[...] + part

    @pl.when((act_ref[i] == 0) & (f == 0))
    def _():
        o_ref[...] = jnp.zeros_like(o_ref)


def _moe_gmm(tile_expert, tile_active, xs, w1, w3, w2):
    n_tiles = tile_expert.shape[0]
    d_ff = w1.shape[2]
    grid_spec = pltpu.PrefetchScalarGridSpec(
        num_scalar_prefetch=2,
        grid=(n_tiles, d_ff // MOE_TF),
        in_specs=[pl.BlockSpec((MOE_TM, D_MODEL), lambda i, f, te, act: (i * act[i], 0)),
                  pl.BlockSpec((1, D_MODEL, MOE_TF), lambda i, f, te, act: (te[i], 0, f)),
                  pl.BlockSpec((1, D_MODEL, MOE_TF), lambda i, f, te, act: (te[i], 0, f)),
                  pl.BlockSpec((1, MOE_TF, D_MODEL), lambda i, f, te, act: (te[i], f, 0))],
        out_specs=pl.BlockSpec((MOE_TM, D_MODEL), lambda i, f, te, act: (i, 0)),
    )
    return pl.pallas_call(
        _moe_kernel,
        grid_spec=grid_spec,
        out_shape=jax.ShapeDtypeStruct((n_tiles * MOE_TM, D_MODEL), f32),
        compiler_params=_cparams(("arbitrary", "arbitrary")),
        name="moe_gmm",
    )(tile_expert, tile_active, xs, w1, w3, w2)


def _combine_kernel(ra_ref, rb_ref, ra_next, rb_next, ga_ref, gb_ref, x_ref, y_hbm, g_ref, b_ref, o_ref,
                    ybuf, sem):
    i = pl.program_id(0)
    slot = lax.rem(i, 2)

    def gather(ia_ref, ib_ref, s):
        def issue(r, carry):
            _row_copy(y_hbm, ia_ref[0, 0, r], ybuf.at[s, 0], r, sem.at[s, 0]).start()
            _row_copy(y_hbm, ib_ref[0, 0, r], ybuf.at[s, 1], r, sem.at[s, 1]).start()
            return carry

        lax.fori_loop(0, COMBINE_TM, issue, 0, unroll=ISSUE_UNROLL)

    @pl.when(i == 0)
    def _():
        gather(ra_ref, rb_ref, 0)

    @pl.when(i + 1 < pl.num_programs(0))
    def _():
        gather(ra_next, rb_next, 1 - slot)

    for k in range(TOP_K):
        pltpu.make_async_copy(y_hbm.at[pl.ds(0, COMBINE_TM), :], ybuf.at[slot, k], sem.at[slot, k]).wait()
    f = ga_ref[...] * ybuf[slot, 0] + gb_ref[...] * ybuf[slot, 1]
    o_ref[...] = _layer_norm_rows(DEEPNORM_ALPHA * x_ref[...] + f, g_ref[...], b_ref[...])


def _combine(ra, rb, ga, gb, x, ys, ln_g, ln_b):
    n = x.shape[0]
    last = n // COMBINE_TM - 1
    smem = pl.BlockSpec((1, 1, COMBINE_TM), lambda i: (i, 0, 0), memory_space=pltpu.SMEM)
    smem_next = pl.BlockSpec((1, 1, COMBINE_TM), lambda i: (jnp.minimum(i + 1, last), 0, 0),
                             memory_space=pltpu.SMEM)
    col = pl.BlockSpec((COMBINE_TM, 1), lambda i: (i, 0))
    row = pl.BlockSpec((COMBINE_TM, D_MODEL), lambda i: (i, 0))
    return pl.pallas_call(
        _combine_kernel,
        grid=(n // COMBINE_TM,),
        in_specs=[smem, smem, smem_next, smem_next, col, col, row, pl.BlockSpec(memory_space=pl.ANY),
                  _const_spec((1, D_MODEL)), _const_spec((1, D_MODEL))],
        out_specs=row,
        out_shape=jax.ShapeDtypeStruct((n, D_MODEL), f32),
        scratch_shapes=[pltpu.VMEM((2, TOP_K, COMBINE_TM, D_MODEL), f32), pltpu.SemaphoreType.DMA((2, TOP_K))],
        compiler_params=_cparams(("arbitrary",)),
        name="moe_combine",
    )(ra, rb, ra, rb, ga, gb, x, ys, ln_g, ln_b)


def _moe_plan(sel, gates):
    n = sel.shape[0]
    rows_total = TOP_K * n + N_EXPERTS * MOE_TM
    n_tiles = rows_total // MOE_TM
    seli = sel.astype(jnp.int32)
    cnt = jnp.sum(seli, axis=0)
    pos = jnp.cumsum(seli, axis=0) - seli
    padded = ((cnt + MOE_TM - 1) // MOE_TM) * MOE_TM
    bounds = jnp.cumsum(padded)
    row = (bounds - padded)[None, :] + pos
    pad_start = jnp.concatenate([bounds - padded + cnt, bounds[-1:]]).astype(jnp.int32)
    pad_cnt = jnp.concatenate([padded - cnt, (rows_total - bounds[-1:]) // MOE_TM]).astype(jnp.int32)
    tile_start = jnp.arange(n_tiles, dtype=jnp.int32) * MOE_TM
    last_used = jnp.max(jnp.where(cnt > 0, jnp.arange(N_EXPERTS, dtype=jnp.int32), 0))
    tile_expert = jnp.sum((tile_start[:, None] >= bounds[None, :]).astype(jnp.int32), axis=1)
    tile_expert = jnp.minimum(tile_expert, last_used)
    tile_active = (tile_start < bounds[-1]).astype(jnp.int32)
    eid = jnp.arange(N_EXPERTS, dtype=jnp.int32)[None, :]
    is_a = eid == jnp.min(jnp.where(sel, eid, N_EXPERTS), axis=1, keepdims=True)
    is_b = eid == jnp.max(jnp.where(sel, eid, -1), axis=1, keepdims=True)
    pick = lambda a, m: jnp.sum(jnp.where(m, a, 0), axis=1)
    return (pad_start, pad_cnt, tile_expert, tile_active,
            pick(row, is_a).astype(jnp.int32), pick(row, is_b).astype(jnp.int32),
            pick(gates, is_a)[:, None], pick(gates, is_b)[:, None])


def kernel(x, s5_w_in, s5_a_re, s5_a_im, s5_log_dt, s5_b_re, s5_b_im, s5_c_re, s5_c_im, s5_d, s5_w_glu,
           attn_w_qkv, attn_w_o, rel_bias, ffn_w1, ffn_w3, ffn_w2, moe_w_router, moe_w1, moe_w3, moe_w2,
           ln_g, ln_b):
    batch, seq, d = x.shape
    n = batch * seq
    assert d == D_MODEL
    assert seq % (DILATED_PAIRS[-1][1] * ATTN_BLOCK * ATT_QB) == 0
    assert seq % S5_TB == 0 and seq % FFN_TM == 0 and FFN_TM == MRG_TM
    vec = lambda v: v.reshape(1, D_MODEL).astype(f32)

    bre, bim, cre, cimn, tab = _s5_tables(s5_a_re[0], s5_a_im[0], s5_log_dt[0], s5_b_re[0], s5_b_im[0],
                                          s5_c_re[0], s5_c_im[0])
    x1 = _s5_layer(x.reshape(n, d), s5_w_in[0].astype(bf16), bre, bim, cre, cimn, tab, vec(s5_d[0]),
                   s5_w_glu[0].astype(bf16), vec(ln_g[0, 0]), vec(ln_b[0, 0]), batch)
    x2, *xgs = _ffn_layer(x1, ffn_w1[0].astype(bf16), ffn_w3[0].astype(bf16), ffn_w2[0].astype(bf16),
                          vec(ln_g[0, 1]), vec(ln_b[0, 1]), batch)

    wq = attn_w_qkv[0].reshape(d, N_DIL_GROUPS, 3 * d).astype(bf16)
    bias = _attn_bias(rel_bias)
    os_, lses = [], []
    for g, (_, dil) in enumerate(DILATED_PAIRS):
        qkv = _qkv_proj(xgs[g].reshape(n, d), wq[:, g])
        o, lse = _attention(qkv, bias[g], seq // (dil * ATTN_BLOCK))
        os_.append(o if g == 0 else o.reshape(batch, dil, seq // dil, d))
        lses.append(lse if g == 0 else lse.reshape(batch, dil, seq // dil, LANE))
    expand = (jnp.arange(LANE)[:, None] == (jnp.arange(D_MODEL)[None, :] // HEAD_DIM)).astype(bf16)
    wr = jnp.pad(moe_w_router[0].astype(f32), ((0, 0), (0, LANE - N_EXPERTS)))
    wr_hi, wr_lo = _split_bf16(wr)
    x3, gates, sel = _merge_layer(os_, lses, x2, expand, attn_w_o[0].astype(bf16), vec(ln_g[1, 0]),
                                  vec(ln_b[1, 0]), wr_hi, wr_lo, batch)

    pad_start, pad_cnt, tile_expert, tile_active, ra, rb, ga, gb = _moe_plan(sel[:, :N_EXPERTS] > 0.5,
                                                                             gates[:, :N_EXPERTS])
    xs = _dispatch_rows(pad_start, pad_cnt, ra.reshape(-1, 1, GATHER_TM), rb.reshape(-1, 1, GATHER_TM), x3,
                        TOP_K * n + N_EXPERTS * MOE_TM)
    ys = _moe_gmm(tile_expert, tile_active, xs, moe_w1[0].astype(bf16), moe_w3[0].astype(bf16),
                  moe_w2[0].astype(bf16))
    out = _combine(ra.reshape(-1, 1, COMBINE_TM), rb.reshape(-1, 1, COMBINE_TM), ga, gb, x3, ys,
                   vec(ln_g[1, 1]), vec(ln_b[1, 1]))
    return out.reshape(batch, seq, d)
```

```python
import functools
import math

import numpy as np
import jax
import jax.numpy as jnp
from jax import lax
from jax.experimental import pallas as pl
from jax.experimental.pallas import tpu as pltpu

D_MODEL = 1024
S5_GROUP = 16
S5_GROUPS = D_MODEL // S5_GROUP
S5_STATE = 64
S5_WIDTH = S5_GROUPS * S5_STATE
HEAD_DIM = 64
N_HEADS = D_MODEL // HEAD_DIM
DILATED_PAIRS = ((128, 1), (512, 4), (2048, 16))
N_DIL_GROUPS = len(DILATED_PAIRS)
ATTN_BLOCK = 128
N_BUCKETS = 32
MAX_DISTANCE = 2048
N_EXPERTS = 8
TOP_K = 2
DEPTH = 2
DEEPNORM_ALPHA = (2 * DEPTH) ** 0.25
LN_EPS = 1e-5
NEG_INF = -1e30
LOG2_E = math.log2(math.e)

LANE = 128
SUBLANE = 8
BF16_SUBLANES = 16
MXU_DIM = 256
VMEM_LIMIT = 56 * 1024 * 1024

S5_T = 32
S5_TB = SUBLANE * S5_T
S5_LC = 512
S5_KT = MXU_DIM
S5_NT = D_MODEL // S5_KT
S5_ST = S5_WIDTH // S5_NT
FFN_TM = 512
QKV_TM = 1024
ATT_QB = 8
ATT_ROWS = 32
MRG_TM = 512
MOE_TM = 512
MOE_TF = 1792
GATHER_TM = 1024
COMBINE_TM = 512
ISSUE_UNROLL = 8

f32 = jnp.float32
bf16 = jnp.bfloat16


def _cparams(sem):
    return pltpu.CompilerParams(dimension_semantics=sem, vmem_limit_bytes=VMEM_LIMIT)


def _const_spec(shape):
    nd = len(shape)
    return pl.BlockSpec(shape, lambda *_: (0,) * nd, pipeline_mode=pl.Buffered(1))


def _layer_norm_rows(v, g, b):
    mu = jnp.mean(v, axis=-1, keepdims=True)
    c = v - mu
    var = jnp.mean(c * c, axis=-1, keepdims=True)
    return c * lax.rsqrt(var + LN_EPS) * g + b


def _gelu_tanh(x):
    return 0.5 * x * (1.0 + jnp.tanh(math.sqrt(2.0 / math.pi) * (x + 0.044715 * (x * x * x))))


def _dot(a, b):
    return jnp.dot(a, b, preferred_element_type=f32)


_TAB_A = 0
_TAB_P1 = 2
_TAB_P2 = 4
_TAB_P4 = 6
_TAB_PC_RE = 8
_TAB_PC_IM = 16
_TAB_ROWS = 24


def _split3_bf16(v):
    hi = v.astype(bf16)
    r1 = v - hi.astype(f32)
    mid = r1.astype(bf16)
    return hi, mid, (r1 - mid.astype(f32)).astype(bf16)


def _permute_rows_f32(perm, v):
    hi, mid, lo = _split3_bf16(v)
    return _dot(perm, hi) + _dot(perm, mid) + _dot(perm, lo)


def _s5_kernel(x_ref, pf_ref, pb_ref, win_ref, bre_ref, bim_ref, cre_ref, cimn_ref, tab_ref, d_ref, wglu_ref,
               g_ref, b_ref, o_ref, u_ref, hre_ref, him_ref, car_re, car_im):
    @pl.when(pl.program_id(0) == 0)
    def _():
        car_re[...] = jnp.zeros_like(car_re)
        car_im[...] = jnp.zeros_like(car_im)

    nb = x_ref.shape[0]
    phases = [_s5_tile(x_ref[b], functools.partial(o_ref.__setitem__, b), pf_ref, pb_ref, win_ref, bre_ref,
                       bim_ref, cre_ref, cimn_ref, tab_ref, d_ref, wglu_ref, g_ref, b_ref, u_ref.at[b],
                       hre_ref.at[b], him_ref.at[b], car_re.at[b], car_im.at[b]) for b in range(nb)]
    _interleave(phases[0][0], [])
    for b in range(nb):
        mxu_work = (phases[b - 1][2] if b > 0 else []) + (phases[b + 1][0] if b + 1 < nb else [])
        _interleave(phases[b][1], mxu_work)
    _interleave(phases[nb - 1][2], [])


def _interleave(a, b):
    done = 0
    for i, piece in enumerate(a):
        piece()
        while done < (i + 1) * len(b) // len(a):
            b[done]()
            done += 1


def _s5_tile(x, emit, pf_ref, pb_ref, win_ref, bre_ref, bim_ref, cre_ref, cimn_ref, tab_ref, d_ref, wglu_ref,
             g_ref, b_ref, u_ref, hre_ref, him_ref, car_re, car_im):
    T = S5_T

    def project_u():
        xs = _dot(pf_ref[...], x.astype(bf16)).astype(bf16)
        u_ref[...] = _dot(xs, win_ref[...])

    def project_b(kt):
        ub = u_ref[:, kt * S5_KT:(kt + 1) * S5_KT].astype(bf16)
        hre_ref[:, kt * S5_ST:(kt + 1) * S5_ST] = _dot(ub, bre_ref[kt])
        him_ref[:, kt * S5_ST:(kt + 1) * S5_ST] = _dot(ub, bim_ref[kt])

    def bcast(r, sl):
        return jnp.broadcast_to(tab_ref[r:r + 1, sl], (SUBLANE, S5_LC))

    def cmul(ar, ai, xr, xi):
        return ar * xr - ai * xi, ar * xi + ai * xr

    def recur(lc):
        row = lax.broadcasted_iota(jnp.int32, (SUBLANE, S5_LC), 0)
        sl = slice(lc * S5_LC, (lc + 1) * S5_LC)
        ar, ai = bcast(_TAB_A, sl), bcast(_TAB_A + 1, sl)

        er, ei = hre_ref[0:SUBLANE, sl], him_ref[0:SUBLANE, sl]
        for t in range(1, T):
            rows = slice(t * SUBLANE, (t + 1) * SUBLANE)
            pr, pi = cmul(ar, ai, er, ei)
            er = pr + hre_ref[rows, sl]
            ei = pi + him_ref[rows, sl]
            hre_ref[rows, sl] = er
            him_ref[rows, sl] = ei

        xr, xi = er, ei
        for s, tr in ((1, _TAB_P1), (2, _TAB_P2), (4, _TAB_P4)):
            sr = jnp.where(row >= s, pltpu.roll(xr, s, 0), 0.0)
            si = jnp.where(row >= s, pltpu.roll(xi, s, 0), 0.0)
            mr, mi = cmul(bcast(tr, sl), bcast(tr + 1, sl), sr, si)
            xr, xi = xr + mr, xi + mi
        shr = jnp.where(row >= 1, pltpu.roll(xr, 1, 0), 0.0)
        shi = jnp.where(row >= 1, pltpu.roll(xi, 1, 0), 0.0)
        cr, ci = car_re[:, sl], car_im[:, sl]
        pcr = tab_ref[_TAB_PC_RE:_TAB_PC_RE + SUBLANE, sl]
        pci = tab_ref[_TAB_PC_IM:_TAB_PC_IM + SUBLANE, sl]
        hr0, hi0 = cmul(pcr, pci, cr, ci)
        hr0, hi0 = hr0 + shr, hi0 + shi
        nr, ni = cmul(bcast(_TAB_P1, sl), bcast(_TAB_P1 + 1, sl), hr0, hi0)
        nr, ni = nr + er, ni + ei
        car_re[:, sl] = jnp.broadcast_to(nr[SUBLANE - 1:SUBLANE], (SUBLANE, S5_LC))
        car_im[:, sl] = jnp.broadcast_to(ni[SUBLANE - 1:SUBLANE], (SUBLANE, S5_LC))

        zr, zi = hr0, hi0
        for t in range(T):
            rows = slice(t * SUBLANE, (t + 1) * SUBLANE)
            zr, zi = cmul(ar, ai, zr, zi)
            hre_ref[rows, sl] = hre_ref[rows, sl] + zr
            him_ref[rows, sl] = him_ref[rows, sl] + zi

    gs, hs = [], []

    def project_c(nt):
        hr = hre_ref[:, nt * S5_ST:(nt + 1) * S5_ST].astype(bf16)
        hi = him_ref[:, nt * S5_ST:(nt + 1) * S5_ST].astype(bf16)
        csl = slice(nt * S5_KT, (nt + 1) * S5_KT)
        y = _dot(hr, cre_ref[nt]) + _dot(hi, cimn_ref[nt]) + d_ref[:, csl] * u_ref[:, csl]
        gs.append(_gelu_tanh(y).astype(bf16))

    def glu(j):
        g = jnp.concatenate(gs, axis=1)
        val = _dot(g, wglu_ref[:, j * S5_KT:(j + 1) * S5_KT])
        gate = _dot(g, wglu_ref[:, D_MODEL + j * S5_KT:D_MODEL + (j + 1) * S5_KT])
        hs.append(val * jax.nn.sigmoid(gate))

    def finish():
        h = _permute_rows_f32(pb_ref[...], jnp.concatenate(hs, axis=1))
        emit(_layer_norm_rows(DEEPNORM_ALPHA * x + h, g_ref[...], b_ref[...]))

    part = functools.partial
    return ([project_u] + [part(project_b, kt) for kt in range(S5_NT)],
            [part(recur, lc) for lc in range(S5_WIDTH // S5_LC)],
            [part(project_c, nt) for nt in range(S5_NT)] + [part(glu, j) for j in range(S5_NT)] + [finish])


def _s5_tables(a_re, a_im, log_dt, b_re, b_im, c_re, c_im):
    dt = jnp.exp(log_dt.astype(f32))[:, None]
    lam_re = jnp.minimum(a_re.astype(f32), -1e-4)
    lam_im = a_im.astype(f32)
    mag = jnp.exp(lam_re * dt)
    ab_re = mag * jnp.cos(lam_im * dt)
    ab_im = mag * jnp.sin(lam_im * dt)
    den = lam_re * lam_re + lam_im * lam_im
    nr = ab_re - 1.0
    q_re = (nr * lam_re + ab_im * lam_im) / den
    q_im = (ab_im * lam_re - nr * lam_im) / den
    br, bi = b_re.astype(f32), b_im.astype(f32)
    bb_re = q_re[..., None] * br - q_im[..., None] * bi
    bb_im = q_re[..., None] * bi + q_im[..., None] * br

    gpt = S5_KT // S5_GROUP
    eye = jnp.eye(gpt, dtype=f32)

    def b_blocks(bb):
        t = bb.reshape(S5_NT, gpt, S5_STATE, S5_GROUP)
        blk = jnp.einsum('ngpc,gh->ngchp', t, eye)
        return blk.reshape(S5_NT, S5_KT, S5_ST).astype(bf16)

    def c_blocks(cc):
        t = cc.reshape(S5_NT, gpt, S5_GROUP, S5_STATE)
        blk = jnp.einsum('ngcp,gh->ngphc', t, eye)
        return blk.reshape(S5_NT, S5_ST, S5_KT).astype(bf16)

    def csq(z):
        return z[0] * z[0] - z[1] * z[1], 2.0 * z[0] * z[1]

    def cmul(x, y):
        return x[0] * y[0] - x[1] * y[1], x[0] * y[1] + x[1] * y[0]

    a1 = (ab_re.reshape(-1), ab_im.reshape(-1))
    p = a1
    for _ in range(int(math.log2(S5_T))):
        p = csq(p)
    p1 = p
    p2 = csq(p1)
    p4 = csq(p2)
    pc = [(jnp.ones_like(p1[0]), jnp.zeros_like(p1[0]))]
    for _ in range(SUBLANE - 1):
        pc.append(cmul(pc[-1], p1))
    rows = [a1[0], a1[1], p1[0], p1[1], p2[0], p2[1], p4[0], p4[1]]
    rows += [z[0] for z in pc] + [z[1] for z in pc]
    tab = jnp.stack(rows, axis=0)
    return (b_blocks(bb_re), b_blocks(bb_im), c_blocks(c_re.astype(f32)), c_blocks(-c_im.astype(f32)), tab)


def _scan_perm():
    p = np.zeros((S5_TB, S5_TB), np.float32)
    for c in range(SUBLANE):
        for t in range(S5_T):
            p[t * SUBLANE + c, c * S5_T + t] = 1.0
    return p


def _perm_rows(dilation):
    return max(LANE, BF16_SUBLANES * dilation)


def _stride_perm(tile, dilation):
    p = np.zeros((tile, tile), np.float32)
    per = tile // dilation
    for j in range(per):
        for r in range(dilation):
            p[r * per + j, j * dilation + r] = 1.0
    return p


def _s5_layer(x, w_in, bre, bim, cre, cimn, tab, d_skip, w_glu, ln_g, ln_b, batch):
    n = x.shape[0]
    seq = n // batch
    row_spec = pl.BlockSpec((batch, S5_TB, D_MODEL), lambda t: (0, t, 0))
    perm = _scan_perm()
    out = pl.pallas_call(
        _s5_kernel,
        grid=(seq // S5_TB,),
        in_specs=[row_spec,
                  _const_spec((S5_TB, S5_TB)), _const_spec((S5_TB, S5_TB)),
                  _const_spec((D_MODEL, D_MODEL)),
                  _const_spec((S5_NT, S5_KT, S5_ST)), _const_spec((S5_NT, S5_KT, S5_ST)),
                  _const_spec((S5_NT, S5_ST, S5_KT)), _const_spec((S5_NT, S5_ST, S5_KT)),
                  _const_spec((_TAB_ROWS, S5_WIDTH)),
                  _const_spec((1, D_MODEL)),
                  _const_spec((D_MODEL, 2 * D_MODEL)),
                  _const_spec((1, D_MODEL)), _const_spec((1, D_MODEL))],
        out_specs=row_spec,
        out_shape=jax.ShapeDtypeStruct((batch, seq, D_MODEL), f32),
        scratch_shapes=[pltpu.VMEM((batch, S5_TB, D_MODEL), f32),
                        pltpu.VMEM((batch, S5_TB, S5_WIDTH), f32), pltpu.VMEM((batch, S5_TB, S5_WIDTH), f32),
                        pltpu.VMEM((batch, SUBLANE, S5_WIDTH), f32),
                        pltpu.VMEM((batch, SUBLANE, S5_WIDTH), f32)],
        compiler_params=_cparams(("arbitrary",)),
        name="s5_mixer",
    )(x.reshape(batch, seq, D_MODEL), jnp.asarray(perm, bf16), jnp.asarray(perm.T, bf16), w_in, bre, bim, cre,
      cimn, tab, d_skip, w_glu, ln_g, ln_b)
    return out.reshape(n, D_MODEL)


def _ffn_kernel(x_ref, w1_ref, w3_ref, w2_ref, g_ref, b_ref, *rest):
    perm_refs = rest[:N_DIL_GROUPS - 1]
    o_ref = rest[N_DIL_GROUPS - 1]
    xg_refs = rest[N_DIL_GROUPS:]
    x = x_ref[...]
    xb = x.astype(bf16)
    h = jax.nn.silu(_dot(xb, w1_ref[...])) * _dot(xb, w3_ref[...])
    f = _dot(h.astype(bf16), w2_ref[...])
    y = _layer_norm_rows(DEEPNORM_ALPHA * x + f, g_ref[...], b_ref[...])
    o_ref[...] = y
    yb = y.astype(bf16)
    xg_refs[0][...] = yb
    for perm_ref, xg_ref in zip(perm_refs, xg_refs[1:]):
        dil = xg_ref.shape[1]
        blk = perm_ref.shape[0]
        piece = blk // dil
        for s in range(FFN_TM // blk):
            v = _dot(perm_ref[...], yb[s * blk:(s + 1) * blk]).astype(bf16)
            xg_ref[0, :, s * piece:(s + 1) * piece, :] = v.reshape(dil, piece, D_MODEL)


def _strided_spec(tile, dilation, width, tiles_per_seq):
    return pl.BlockSpec((1, dilation, tile // dilation, width),
                        lambda i: (i // tiles_per_seq, 0, lax.rem(i, tiles_per_seq), 0))


def _ffn_layer(x, w1, w3, w2, ln_g, ln_b, batch):
    n = x.shape[0]
    seq = n // batch
    d_ff = w1.shape[1]
    row_spec = pl.BlockSpec((FFN_TM, D_MODEL), lambda i: (i, 0))
    dils = [dil for _, dil in DILATED_PAIRS[1:]]
    perms = [jnp.asarray(_stride_perm(_perm_rows(dil), dil), bf16) for dil in dils]
    return pl.pallas_call(
        _ffn_kernel,
        grid=(n // FFN_TM,),
        in_specs=[row_spec, _const_spec((D_MODEL, d_ff)), _const_spec((D_MODEL, d_ff)),
                  _const_spec((d_ff, D_MODEL)), _const_spec((1, D_MODEL)), _const_spec((1, D_MODEL))]
                 + [_const_spec(p.shape) for p in perms],
        out_specs=[row_spec, row_spec] + [_strided_spec(FFN_TM, dil, D_MODEL, seq // FFN_TM) for dil in dils],
        out_shape=[jax.ShapeDtypeStruct((n, D_MODEL), f32), jax.ShapeDtypeStruct((n, D_MODEL), bf16)]
                  + [jax.ShapeDtypeStruct((batch, dil, seq // dil, D_MODEL), bf16) for dil in dils],
        compiler_params=_cparams(("arbitrary",)),
        name="dense_ffn",
    )(x, w1, w3, w2, ln_g, ln_b, *perms)


def _qkv_kernel(x_ref, w_ref, o_ref):
    x = x_ref[...]
    for j in range(3):
        cs = slice(j * D_MODEL, (j + 1) * D_MODEL)
        r = _dot(x, w_ref[:, cs])
        if j == 0:
            r = r * (LOG2_E / math.sqrt(HEAD_DIM))
        o_ref[:, cs] = r.astype(o_ref.dtype)


def _qkv_proj(xg, wg):
    n = xg.shape[0]
    return pl.pallas_call(
        _qkv_kernel,
        grid=(n // QKV_TM,),
        in_specs=[pl.BlockSpec((QKV_TM, D_MODEL), lambda i: (i, 0)), _const_spec((D_MODEL, 3 * D_MODEL))],
        out_specs=pl.BlockSpec((QKV_TM, 3 * D_MODEL), lambda i: (i, 0)),
        out_shape=jax.ShapeDtypeStruct((n, 3 * D_MODEL), bf16),
        compiler_params=_cparams(("arbitrary",)),
        name="qkv_proj",
    )(xg, wg)


def _t5_bucket(dist):
    max_exact = N_BUCKETS // 2
    d = np.maximum(dist, 0)
    large = max_exact + (np.log(np.maximum(d, 1) / max_exact) / math.log(MAX_DISTANCE / max_exact)
                         * (N_BUCKETS - max_exact)).astype(np.int64)
    large = np.minimum(large, N_BUCKETS - 1)
    return np.where(d < max_exact, d, large).astype(np.int32)


def _attn_bias(rel_bias):
    L = ATTN_BLOCK
    ki = np.arange(2 * L)[:, None]
    qi = np.arange(L)[None, :]
    off = qi + L - ki
    out = []
    for window, dilation in DILATED_PAIRS:
        band = (off >= 0) & (off <= window // dilation)
        bucket = _t5_bucket(np.clip(off, 0, None) * dilation)
        onehot = jnp.asarray(bucket[None] == np.arange(N_BUCKETS)[:, None, None], f32)
        bias = jnp.einsum('bkl,bh->hkl', onehot, rel_bias.astype(f32), precision=lax.Precision.HIGHEST)
        out.append(jnp.where(band[None], bias * LOG2_E, NEG_INF))
    return jnp.stack(out, axis=0)


def _attn_kernel(q_ref, kp_ref, kc_ref, vp_ref, vc_ref, bias_ref, o_ref, lse_ref, *, blocks_per_seq):
    L = ATTN_BLOCK
    low = lax.broadcasted_iota(jnp.int32, (L, LANE), 1) < HEAD_DIM
    head_row = lax.broadcasted_iota(jnp.int32, (N_HEADS, L), 0)
    is_first = lax.rem(pl.program_id(0) * ATT_QB, blocks_per_seq) == 0
    key_row = lax.broadcasted_iota(jnp.int32, (2 * L, L), 0)
    pen = jnp.where((key_row < L) & is_first, NEG_INF, 0.0).astype(f32)
    for s in range(ATT_QB):
        rs = slice(s * L, (s + 1) * L)
        ks = slice((s - 1) * L, (s + 1) * L)
        lse_t = jnp.zeros((N_HEADS, L), f32)
        for hp in range(N_HEADS // 2):
            cs = slice(hp * LANE, (hp + 1) * LANE)
            qp = q_ref[rs, cs]
            if s == 0:
                kpair = jnp.concatenate([kp_ref[:, cs], kc_ref[rs, cs]], axis=0)
                vpair = jnp.concatenate([vp_ref[:, cs], vc_ref[rs, cs]], axis=0)
            else:
                kpair, vpair = kc_ref[ks, cs], vc_ref[ks, cs]
            outs = []
            for half in range(2):
                h = 2 * hp + half
                qm = jnp.where(low if half == 0 else jnp.logical_not(low), qp, jnp.zeros_like(qp))
                logits = lax.dot_general(kpair, qm, (((1,), (1,)), ((), ())),
                                         preferred_element_type=f32)
                logits = logits + bias_ref[h]
                if s == 0:
                    logits = logits + pen
                m = jnp.max(logits, axis=0, keepdims=True)
                p = jnp.exp2(logits - m)
                ssum = jnp.sum(p, axis=0, keepdims=True)
                pn = (p * (1.0 / ssum)).astype(bf16)
                outs.append(lax.dot_general(pn, vpair, (((0,), (0,)), ((), ())),
                                            preferred_element_type=f32))
                lse_t = jnp.where(head_row == h, (m + jnp.log2(ssum)) * math.log(2.0), lse_t)
            o_ref[rs, cs] = jnp.where(low, outs[0], outs[1]).astype(o_ref.dtype)
        lse_rows = jnp.concatenate([lse_t, jnp.zeros((LANE - N_HEADS, L), f32)], axis=0)
        lse_ref[rs, :] = lse_rows.T


def _attention(qkv, bias, blocks_per_seq):
    n = qkv.shape[0]
    L = ATTN_BLOCK
    tq = ATT_QB * L
    assert blocks_per_seq % ATT_QB == 0

    def cur(col):
        return pl.BlockSpec((tq, D_MODEL), lambda i: (i, col))

    def prev(col):
        return pl.BlockSpec((L, D_MODEL), lambda i: (jnp.maximum(i * ATT_QB - 1, 0), col))

    return pl.pallas_call(
        functools.partial(_attn_kernel, blocks_per_seq=blocks_per_seq),
        grid=(n // tq,),
        in_specs=[cur(0), prev(1), cur(1), prev(2), cur(2), _const_spec((N_HEADS, 2 * L, L))],
        out_specs=[pl.BlockSpec((tq, D_MODEL), lambda i: (i, 0)), pl.BlockSpec((tq, LANE), lambda i: (i, 0))],
        out_shape=[jax.ShapeDtypeStruct((n, D_MODEL), bf16), jax.ShapeDtypeStruct((n, LANE), f32)],
        compiler_params=_cparams(("arbitrary",)),
        name="dilated_attn",
    )(qkv, qkv, qkv, qkv, qkv, bias)


def _split_bf16(v):
    hi = v.astype(bf16)
    return hi, (v - hi.astype(f32)).astype(bf16)


def _unstride_rows(ref, perm_ref, apply):
    dil, per, width = ref.shape[1:]
    blk = perm_ref.shape[0]
    piece = blk // dil
    outs = []
    for s in range(dil * per // blk):
        v = ref[0, :, s * piece:(s + 1) * piece, :].reshape(blk, width)
        outs.append(apply(perm_ref[...], v))
    return jnp.concatenate(outs, axis=0)


def _merge_kernel(*refs):
    ng = N_DIL_GROUPS
    o_refs, lse_refs, perm_refs = refs[:ng], refs[ng:2 * ng], refs[2 * ng:3 * ng - 1]
    (x_ref, e_ref, wo_ref, g_ref, b_ref, wrh_ref, wrl_ref, x3_ref, gate_ref, sel_ref) = refs[3 * ng - 1:]
    os_ = [o_refs[0][...].astype(f32)]
    ls = [lse_refs[0][...]]
    for o_ref, lse_ref, perm_ref in zip(o_refs[1:], lse_refs[1:], perm_refs):
        os_.append(_unstride_rows(o_ref, perm_ref, lambda p, v: _dot(p, v)))
        ls.append(_unstride_rows(lse_ref, perm_ref, _permute_rows_f32))
    mx = functools.reduce(jnp.maximum, ls)
    es = [jnp.exp(l - mx) for l in ls]
    den = functools.reduce(jnp.add, es)
    merged = None
    for g in range(ng):
        wfull = _dot((es[g] / den).astype(bf16), e_ref[...])
        term = wfull * os_[g]
        merged = term if merged is None else merged + term
    att = _dot(merged.astype(bf16), wo_ref[...])
    x3 = _layer_norm_rows(DEEPNORM_ALPHA * x_ref[...] + att, g_ref[...], b_ref[...])
    x3_ref[...] = x3

    xh, xl = _split_bf16(x3)
    logits = _dot(xh, wrh_ref[...]) + _dot(xl, wrh_ref[...]) + _dot(xh, wrl_ref[...])
    lane = lax.broadcasted_iota(jnp.int32, logits.shape, 1)
    neg = -jnp.inf
    logits = jnp.where(lane < N_EXPERTS, logits, neg)
    m1 = jnp.max(logits, axis=-1, keepdims=True)
    i1 = jnp.min(jnp.where(logits == m1, lane, LANE), axis=-1, keepdims=True)
    pick1 = lane == i1
    rest = jnp.where(pick1, neg, logits)
    m2 = jnp.max(rest, axis=-1, keepdims=True)
    i2 = jnp.min(jnp.where(rest == m2, lane, LANE), axis=-1, keepdims=True)
    pick2 = lane == i2
    e2 = jnp.exp(m2 - m1)
    gate_ref[...] = jnp.where(pick1, 1.0 / (1.0 + e2), jnp.where(pick2, e2 / (1.0 + e2), 0.0))
    sel_ref[...] = jnp.where(pick1 | pick2, 1.0, 0.0)


def _merge_layer(os_, lses, x, expand, w_o, ln_g, ln_b, wr_hi, wr_lo, batch):
    n = x.shape[0]
    tiles_per_seq = n // batch // MRG_TM
    row = lambda w: pl.BlockSpec((MRG_TM, w), lambda i: (i, 0))
    dils = [dil for _, dil in DILATED_PAIRS[1:]]
    perms = [jnp.asarray(_stride_perm(_perm_rows(dil), dil).T, bf16) for dil in dils]
    return pl.pallas_call(
        _merge_kernel,
        grid=(n // MRG_TM,),
        in_specs=[row(D_MODEL)] + [_strided_spec(MRG_TM, dil, D_MODEL, tiles_per_seq) for dil in dils]
                 + [row(LANE)] + [_strided_spec(MRG_TM, dil, LANE, tiles_per_seq) for dil in dils]
                 + [_const_spec(p.shape) for p in perms]
                 + [row(D_MODEL), _const_spec((LANE, D_MODEL)),
                    _const_spec((D_MODEL, D_MODEL)), _const_spec((1, D_MODEL)), _const_spec((1, D_MODEL)),
                    _const_spec((D_MODEL, LANE)), _const_spec((D_MODEL, LANE))],
        out_specs=[row(D_MODEL), row(LANE), row(LANE)],
        out_shape=[jax.ShapeDtypeStruct((n, D_MODEL), f32), jax.ShapeDtypeStruct((n, LANE), f32),
                   jax.ShapeDtypeStruct((n, LANE), f32)],
        compiler_params=_cparams(("arbitrary",)),
        name="attn_merge",
    )(*os_, *lses, *perms, x, expand, w_o, ln_g, ln_b, wr_hi, wr_lo)


def _row_copy(src_hbm, src_row, dst_ref, dst_row, sem):
    return pltpu.make_async_copy(src_hbm.at[pl.ds(src_row, 1), :], dst_ref.at[pl.ds(dst_row, 1), :], sem)


def _dispatch_kernel(pad_start_ref, pad_cnt_ref, ra_ref, rb_ref, x_ref, xs_hbm, zbuf, sem, zsem):
    def fill_copy(e, k):
        return _row_copy(zbuf, 0, xs_hbm, pad_start_ref[e] + k, zsem)

    def tail_copy(k):
        start = pl.multiple_of(pad_start_ref[N_EXPERTS] + k * MOE_TM, MOE_TM)
        return pltpu.make_async_copy(zbuf, xs_hbm.at[pl.ds(start, MOE_TM), :], zsem)

    @pl.when(pl.program_id(0) == 0)
    def _():
        zbuf[...] = jnp.zeros_like(zbuf)
        for e in range(N_EXPERTS):
            lax.fori_loop(0, pad_cnt_ref[e], lambda k, c: (fill_copy(e, k).start(), c)[1], 0)
        lax.fori_loop(0, pad_cnt_ref[N_EXPERTS], lambda k, c: (tail_copy(k).start(), c)[1], 0)

    def issue(r, carry):
        _row_copy(x_ref, r, xs_hbm, ra_ref[0, 0, r], sem).start()
        _row_copy(x_ref, r, xs_hbm, rb_ref[0, 0, r], sem).start()
        return carry

    lax.fori_loop(0, GATHER_TM, issue, 0, unroll=ISSUE_UNROLL)

    for _ in range(TOP_K):
        pltpu.make_async_copy(x_ref, xs_hbm.at[pl.ds(0, GATHER_TM), :], sem).wait()

    @pl.when(pl.program_id(0) == 0)
    def _():
        for e in range(N_EXPERTS):
            lax.fori_loop(0, pad_cnt_ref[e], lambda k, c: (fill_copy(e, k).wait(), c)[1], 0)
        lax.fori_loop(0, pad_cnt_ref[N_EXPERTS], lambda k, c: (tail_copy(k).wait(), c)[1], 0)


def _dispatch_rows(pad_start, pad_cnt, ra, rb, x, rows_total):
    n = x.shape[0]
    smem = pl.BlockSpec((1, 1, GATHER_TM), lambda i, ps, pc: (i, 0, 0), memory_space=pltpu.SMEM)
    grid_spec = pltpu.PrefetchScalarGridSpec(
        num_scalar_prefetch=2,
        grid=(n // GATHER_TM,),
        in_specs=[smem, smem, pl.BlockSpec((GATHER_TM, D_MODEL), lambda i, ps, pc: (i, 0))],
        out_specs=pl.BlockSpec(memory_space=pl.ANY),
        scratch_shapes=[pltpu.VMEM((MOE_TM, D_MODEL), x.dtype), pltpu.SemaphoreType.DMA(()),
                        pltpu.SemaphoreType.DMA(())],
    )
    return pl.pallas_call(
        _dispatch_kernel,
        grid_spec=grid_spec,
        out_shape=jax.ShapeDtypeStruct((rows_total, D_MODEL), x.dtype),
        compiler_params=_cparams(("arbitrary",)),
        name="moe_dispatch",
    )(pad_start, pad_cnt, ra, rb, x)


def _moe_kernel(te_ref, act_ref, x_ref, w1_ref, w3_ref, w2_ref, o_ref):
    del te_ref
    i, f = pl.program_id(0), pl.program_id(1)

    @pl.when(act_ref[i] == 1)
    def _():
        xb = x_ref[...].astype(bf16)
        h = jax.nn.silu(_dot(xb, w1_ref[0])) * _dot(xb, w3_ref[0])
        part = _dot(h.astype(bf16), w2_ref[0])

        @pl.when(f == 0)
        def _():
            o_ref[...] = part

        @pl.when(f > 0)
        def _():
            o_ref[...] = o_ref[...] + part

    @pl.when((act_ref[i] == 0) & (f == 0))
    def _():
        o_ref[...] = jnp.zeros_like(o_ref)


def _moe_gmm(tile_expert, tile_active, xs, w1, w3, w2):
    n_tiles = tile_expert.shape[0]
    d_ff = w1.shape[2]
    grid_spec = pltpu.PrefetchScalarGridSpec(
        num_scalar_prefetch=2,
        grid=(n_tiles, d_ff // MOE_TF),
        in_specs=[pl.BlockSpec((MOE_TM, D_MODEL), lambda i, f, te, act: (i * act[i], 0)),
                  pl.BlockSpec((1, D_MODEL, MOE_TF), lambda i, f, te, act: (te[i], 0, f)),
                  pl.BlockSpec((1, D_MODEL, MOE_TF), lambda i, f, te, act: (te[i], 0, f)),
                  pl.BlockSpec((1, MOE_TF, D_MODEL), lambda i, f, te, act: (te[i], f, 0))],
        out_specs=pl.BlockSpec((MOE_TM, D_MODEL), lambda i, f, te, act: (i, 0)),
    )
    return pl.pallas_call(
        _moe_kernel,
        grid_spec=grid_spec,
        out_shape=jax.ShapeDtypeStruct((n_tiles * MOE_TM, D_MODEL), f32),
        compiler_params=_cparams(("arbitrary", "arbitrary")),
        name="moe_gmm",
    )(tile_expert, tile_active, xs, w1, w3, w2)


def _combine_kernel(ra_ref, rb_ref, ra_next, rb_next, ga_ref, gb_ref, x_ref, y_hbm, g_ref, b_ref, o_ref,
                    ybuf, sem):
    i = pl.program_id(0)
    slot = lax.rem(i, 2)

    def gather(ia_ref, ib_ref, s):
        def issue(r, carry):
            _row_copy(y_hbm, ia_ref[0, 0, r], ybuf.at[s, 0], r, sem.at[s, 0]).start()
            _row_copy(y_hbm, ib_ref[0, 0, r], ybuf.at[s, 1], r, sem.at[s, 1]).start()
            return carry

        lax.fori_loop(0, COMBINE_TM, issue, 0, unroll=ISSUE_UNROLL)

    @pl.when(i == 0)
    def _():
        gather(ra_ref, rb_ref, 0)

    @pl.when(i + 1 < pl.num_programs(0))
    def _():
        gather(ra_next, rb_next, 1 - slot)

    for k in range(TOP_K):
        pltpu.make_async_copy(y_hbm.at[pl.ds(0, COMBINE_TM), :], ybuf.at[slot, k], sem.at[slot, k]).wait()
    f = ga_ref[...] * ybuf[slot, 0] + gb_ref[...] * ybuf[slot, 1]
    o_ref[...] = _layer_norm_rows(DEEPNORM_ALPHA * x_ref[...] + f, g_ref[...], b_ref[...])


def _combine(ra, rb, ga, gb, x, ys, ln_g, ln_b):
    n = x.shape[0]
    last = n // COMBINE_TM - 1
    smem = pl.BlockSpec((1, 1, COMBINE_TM), lambda i: (i, 0, 0), memory_space=pltpu.SMEM)
    smem_next = pl.BlockSpec((1, 1, COMBINE_TM), lambda i: (jnp.minimum(i + 1, last), 0, 0),
                             memory_space=pltpu.SMEM)
    col = pl.BlockSpec((COMBINE_TM, 1), lambda i: (i, 0))
    row = pl.BlockSpec((COMBINE_TM, D_MODEL), lambda i: (i, 0))
    return pl.pallas_call(
        _combine_kernel,
        grid=(n // COMBINE_TM,),
        in_specs=[smem, smem, smem_next, smem_next, col, col, row, pl.BlockSpec(memory_space=pl.ANY),
                  _const_spec((1, D_MODEL)), _const_spec((1, D_MODEL))],
        out_specs=row,
        out_shape=jax.ShapeDtypeStruct((n, D_MODEL), f32),
        scratch_shapes=[pltpu.VMEM((2, TOP_K, COMBINE_TM, D_MODEL), f32), pltpu.SemaphoreType.DMA((2, TOP_K))],
        compiler_params=_cparams(("arbitrary",)),
        name="moe_combine",
    )(ra, rb, ra, rb, ga, gb, x, ys, ln_g, ln_b)


def _moe_plan(sel, gates):
    n = sel.shape[0]
    rows_total = TOP_K * n + N_EXPERTS * MOE_TM
    n_tiles = rows_total // MOE_TM
    seli = sel.astype(jnp.int32)
    cnt = jnp.sum(seli, axis=0)
    pos = jnp.cumsum(seli, axis=0) - seli
    padded = ((cnt + MOE_TM - 1) // MOE_TM) * MOE_TM
    bounds = jnp.cumsum(padded)
    row = (bounds - padded)[None, :] + pos
    pad_start = jnp.concatenate([bounds - padded + cnt, bounds[-1:]]).astype(jnp.int32)
    pad_cnt = jnp.concatenate([padded - cnt, (rows_total - bounds[-1:]) // MOE_TM]).astype(jnp.int32)
    tile_start = jnp.arange(n_tiles, dtype=jnp.int32) * MOE_TM
    last_used = jnp.max(jnp.where(cnt > 0, jnp.arange(N_EXPERTS, dtype=jnp.int32), 0))
    tile_expert = jnp.sum((tile_start[:, None] >= bounds[None, :]).astype(jnp.int32), axis=1)
    tile_expert = jnp.minimum(tile_expert, last_used)
    tile_active = (tile_start < bounds[-1]).astype(jnp.int32)
    eid = jnp.arange(N_EXPERTS, dtype=jnp.int32)[None, :]
    is_a = eid == jnp.min(jnp.where(sel, eid, N_EXPERTS), axis=1, keepdims=True)
    is_b = eid == jnp.max(jnp.where(sel, eid, -1), axis=1, keepdims=True)
    pick = lambda a, m: jnp.sum(jnp.where(m, a, 0), axis=1)
    return (pad_start, pad_cnt, tile_expert, tile_active,
            pick(row, is_a).astype(jnp.int32), pick(row, is_b).astype(jnp.int32),
            pick(gates, is_a)[:, None], pick(gates, is_b)[:, None])


def kernel(x, s5_w_in, s5_a_re, s5_a_im, s5_log_dt, s5_b_re, s5_b_im, s5_c_re, s5_c_im, s5_d, s5_w_glu,
           attn_w_qkv, attn_w_o, rel_bias, ffn_w1, ffn_w3, ffn_w2, moe_w_router, moe_w1, moe_w3, moe_w2,
           ln_g, ln_b):
    batch, seq, d = x.shape
    n = batch * seq
    assert d == D_MODEL
    assert seq % (DILATED_PAIRS[-1][1] * ATTN_BLOCK * ATT_QB) == 0
    assert seq % S5_TB == 0 and seq % FFN_TM == 0 and FFN_TM == MRG_TM
    vec = lambda v: v.reshape(1, D_MODEL).astype(f32)

    bre, bim, cre, cimn, tab = _s5_tables(s5_a_re[0], s5_a_im[0], s5_log_dt[0], s5_b_re[0], s5_b_im[0],
                                          s5_c_re[0], s5_c_im[0])
    x1 = _s5_layer(x.reshape(n, d), s5_w_in[0].astype(bf16), bre, bim, cre, cimn, tab, vec(s5_d[0]),
                   s5_w_glu[0].astype(bf16), vec(ln_g[0, 0]), vec(ln_b[0, 0]), batch)
    x2, *xgs = _ffn_layer(x1, ffn_w1[0].astype(bf16), ffn_w3[0].astype(bf16), ffn_w2[0].astype(bf16),
                          vec(ln_g[0, 1]), vec(ln_b[0, 1]), batch)

    wq = attn_w_qkv[0].reshape(d, N_DIL_GROUPS, 3 * d).astype(bf16)
    bias = _attn_bias(rel_bias)
    os_, lses = [], []
    for g, (_, dil) in enumerate(DILATED_PAIRS):
        qkv = _qkv_proj(xgs[g].reshape(n, d), wq[:, g])
        o, lse = _attention(qkv, bias[g], seq // (dil * ATTN_BLOCK))
        os_.append(o if g == 0 else o.reshape(batch, dil, seq // dil, d))
        lses.append(lse if g == 0 else lse.reshape(batch, dil, seq // dil, LANE))
    expand = (jnp.arange(LANE)[:, None] == (jnp.arange(D_MODEL)[None, :] // HEAD_DIM)).astype(bf16)
    wr = jnp.pad(moe_w_router[0].astype(f32), ((0, 0), (0, LANE - N_EXPERTS)))
    wr_hi, wr_lo = _split_bf16(wr)
    x3, gates, sel = _merge_layer(os_, lses, x2, expand, attn_w_o[0].astype(bf16), vec(ln_g[1, 0]),
                                  vec(ln_b[1, 0]), wr_hi, wr_lo, batch)

    pad_start, pad_cnt, tile_expert, tile_active, ra, rb, ga, gb = _moe_plan(sel[:, :N_EXPERTS] > 0.5,
                                                                             gates[:, :N_EXPERTS])
    xs = _dispatch_rows(pad_start, pad_cnt, ra.reshape(-1, 1, GATHER_TM), rb.reshape(-1, 1, GATHER_TM), x3,
                        TOP_K * n + N_EXPERTS * MOE_TM)
    ys = _moe_gmm(tile_expert, tile_active, xs, moe_w1[0].astype(bf16), moe_w3[0].astype(bf16),
                  moe_w2[0].astype(bf16))
    out = _combine(ra.reshape(-1, 1, COMBINE_TM), rb.reshape(-1, 1, COMBINE_TM), ga, gb, x3, ys,
                   vec(ln_g[1, 1]), vec(ln_b[1, 1]))
    return out.reshape(batch, seq, d)
```

```python
import functools
import math

import numpy as np
import jax
import jax.numpy as jnp
from jax import lax
from jax.experimental import pallas as pl
from jax.experimental.pallas import tpu as pltpu

D_MODEL = 1024
S5_GROUP = 16
S5_GROUPS = D_MODEL // S5_GROUP
S5_STATE = 64
S5_WIDTH = S5_GROUPS * S5_STATE
HEAD_DIM = 64
N_HEADS = D_MODEL // HEAD_DIM
DILATED_PAIRS = ((128, 1), (512, 4), (2048, 16))
N_DIL_GROUPS = len(DILATED_PAIRS)
ATTN_BLOCK = 128
N_BUCKETS = 32
MAX_DISTANCE = 2048
N_EXPERTS = 8
TOP_K = 2
DEPTH = 2
DEEPNORM_ALPHA = (2 * DEPTH) ** 0.25
LN_EPS = 1e-5
NEG_INF = -1e30
LOG2_E = math.log2(math.e)

LANE = 128
SUBLANE = 8
BF16_SUBLANES = 16
MXU_DIM = 256
VMEM_LIMIT = 56 * 1024 * 1024

S5_T = 32
S5_TB = SUBLANE * S5_T
S5_LC = 512
S5_KT = MXU_DIM
S5_NT = D_MODEL // S5_KT
S5_ST = S5_WIDTH // S5_NT
FFN_TM = 512
QKV_TM = 1024
ATT_QB = 8
ATT_ROWS = 32
MRG_TM = 512
MOE_TM = 512
MOE_TF = 1792
GATHER_TM = 1024
COMBINE_TM = 512
ISSUE_UNROLL = 8

f32 = jnp.float32
bf16 = jnp.bfloat16


def _cparams(sem):
    return pltpu.CompilerParams(dimension_semantics=sem, vmem_limit_bytes=VMEM_LIMIT)


def _const_spec(shape):
    nd = len(shape)
    return pl.BlockSpec(shape, lambda *_: (0,) * nd, pipeline_mode=pl.Buffered(1))


def _layer_norm_rows(v, g, b):
    mu = jnp.mean(v, axis=-1, keepdims=True)
    c = v - mu
    var = jnp.mean(c * c, axis=-1, keepdims=True)
    return c * lax.rsqrt(var + LN_EPS) * g + b


def _gelu_tanh(x):
    return 0.5 * x * (1.0 + jnp.tanh(math.sqrt(2.0 / math.pi) * (x + 0.044715 * (x * x * x))))


def _dot(a, b):
    return jnp.dot(a, b, preferred_element_type=f32)


_TAB_A = 0
_TAB_P1 = 2
_TAB_P2 = 4
_TAB_P4 = 6
_TAB_PC_RE = 8
_TAB_PC_IM = 16
_TAB_ROWS = 24


def _split3_bf16(v):
    hi = v.astype(bf16)
    r1 = v - hi.astype(f32)
    mid = r1.astype(bf16)
    return hi, mid, (r1 - mid.astype(f32)).astype(bf16)


def _permute_rows_f32(perm, v):
    hi, mid, lo = _split3_bf16(v)
    return _dot(perm, hi) + _dot(perm, mid) + _dot(perm, lo)


def _s5_kernel(x_ref, pf_ref, pb_ref, win_ref, bre_ref, bim_ref, cre_ref, cimn_ref, tab_ref, d_ref, wglu_ref,
               g_ref, b_ref, o_ref, u_ref, hre_ref, him_ref, car_re, car_im):
    @pl.when(pl.program_id(0) == 0)
    def _():
        car_re[...] = jnp.zeros_like(car_re)
        car_im[...] = jnp.zeros_like(car_im)

    nb = x_ref.shape[0]
    phases = [_s5_tile(x_ref[b], functools.partial(o_ref.__setitem__, b), pf_ref, pb_ref, win_ref, bre_ref,
                       bim_ref, cre_ref, cimn_ref, tab_ref, d_ref, wglu_ref, g_ref, b_ref, u_ref.at[b],
                       hre_ref.at[b], him_ref.at[b], car_re.at[b], car_im.at[b]) for b in range(nb)]
    _interleave(phases[0][0], [])
    for b in range(nb):
        mxu_work = (phases[b - 1][2] if b > 0 else []) + (phases[b + 1][0] if b + 1 < nb else [])
        _interleave(phases[b][1], mxu_work)
    _interleave(phases[nb - 1][2], [])


def _interleave(a, b):
    done = 0
    for i, piece in enumerate(a):
        piece()
        while done < (i + 1) * len(b) // len(a):
            b[done]()
            done += 1


def _s5_tile(x, emit, pf_ref, pb_ref, win_ref, bre_ref, bim_ref, cre_ref, cimn_ref, tab_ref, d_ref, wglu_ref,
             g_ref, b_ref, u_ref, hre_ref, him_ref, car_re, car_im):
    T = S5_T

    def project_u():
        xs = _dot(pf_ref[...], x.astype(bf16)).astype(bf16)
        u_ref[...] = _dot(xs, win_ref[...])

    def project_b(kt):
        ub = u_ref[:, kt * S5_KT:(kt + 1) * S5_KT].astype(bf16)
        hre_ref[:, kt * S5_ST:(kt + 1) * S5_ST] = _dot(ub, bre_ref[kt])
        him_ref[:, kt * S5_ST:(kt + 1) * S5_ST] = _dot(ub, bim_ref[kt])

    def bcast(r, sl):
        return jnp.broadcast_to(tab_ref[r:r + 1, sl], (SUBLANE, S5_LC))

    def cmul(ar, ai, xr, xi):
        return ar * xr - ai * xi, ar * xi + ai * xr

    def recur(lc):
        row = lax.broadcasted_iota(jnp.int32, (SUBLANE, S5_LC), 0)
        sl = slice(lc * S5_LC, (lc + 1) * S5_LC)
        ar, ai = bcast(_TAB_A, sl), bcast(_TAB_A + 1, sl)

        er, ei = hre_ref[0:SUBLANE, sl], him_ref[0:SUBLANE, sl]
        for t in range(1, T):
            rows = slice(t * SUBLANE, (t + 1) * SUBLANE)
            pr, pi = cmul(ar, ai, er, ei)
            er = pr + hre_ref[rows, sl]
            ei = pi + him_ref[rows, sl]
            hre_ref[rows, sl] = er
            him_ref[rows, sl] = ei

        xr, xi = er, ei
        for s, tr in ((1, _TAB_P1), (2, _TAB_P2), (4, _TAB_P4)):
            sr = jnp.where(row >= s, pltpu.roll(xr, s, 0), 0.0)
            si = jnp.where(row >= s, pltpu.roll(xi, s, 0), 0.0)
            mr, mi = cmul(bcast(tr, sl), bcast(tr + 1, sl), sr, si)
            xr, xi = xr + mr, xi + mi
        shr = jnp.where(row >= 1, pltpu.roll(xr, 1, 0), 0.0)
        shi = jnp.where(row >= 1, pltpu.roll(xi, 1, 0), 0.0)
        cr, ci = car_re[:, sl], car_im[:, sl]
        pcr = tab_ref[_TAB_PC_RE:_TAB_PC_RE + SUBLANE, sl]
        pci = tab_ref[_TAB_PC_IM:_TAB_PC_IM + SUBLANE, sl]
        hr0, hi0 = cmul(pcr, pci, cr, ci)
        hr0, hi0 = hr0 + shr, hi0 + shi
        nr, ni = cmul(bcast(_TAB_P1, sl), bcast(_TAB_P1 + 1, sl), hr0, hi0)
        nr, ni = nr + er, ni + ei
        car_re[:, sl] = jnp.broadcast_to(nr[SUBLANE - 1:SUBLANE], (SUBLANE, S5_LC))
        car_im[:, sl] = jnp.broadcast_to(ni[SUBLANE - 1:SUBLANE], (SUBLANE, S5_LC))

        zr, zi = hr0, hi0
        for t in range(T):
            rows = slice(t * SUBLANE, (t + 1) * SUBLANE)
            zr, zi = cmul(ar, ai, zr, zi)
            hre_ref[rows, sl] = hre_ref[rows, sl] + zr
            him_ref[rows, sl] = him_ref[rows, sl] + zi

    gs, hs = [], []

    def project_c(nt):
        hr = hre_ref[:, nt * S5_ST:(nt + 1) * S5_ST].astype(bf16)
        hi = him_ref[:, nt * S5_ST:(nt + 1) * S5_ST].astype(bf16)
        csl = slice(nt * S5_KT, (nt + 1) * S5_KT)
        y = _dot(hr, cre_ref[nt]) + _dot(hi, cimn_ref[nt]) + d_ref[:, csl] * u_ref[:, csl]
        gs.append(_gelu_tanh(y).astype(bf16))

    def glu(j):
        g = jnp.concatenate(gs, axis=1)
        val = _dot(g, wglu_ref[:, j * S5_KT:(j + 1) * S5_KT])
        gate = _dot(g, wglu_ref[:, D_MODEL + j * S5_KT:D_MODEL + (j + 1) * S5_KT])
        hs.append(val * jax.nn.sigmoid(gate))

    def finish():
        h = _permute_rows_f32(pb_ref[...], jnp.concatenate(hs, axis=1))
        emit(_layer_norm_rows(DEEPNORM_ALPHA * x + h, g_ref[...], b_ref[...]))

    part = functools.partial
    return ([project_u] + [part(project_b, kt) for kt in range(S5_NT)],
            [part(recur, lc) for lc in range(S5_WIDTH // S5_LC)],
            [part(project_c, nt) for nt in range(S5_NT)] + [part(glu, j) for j in range(S5_NT)] + [finish])


def _s5_tables(a_re, a_im, log_dt, b_re, b_im, c_re, c_im):
    dt = jnp.exp(log_dt.astype(f32))[:, None]
    lam_re = jnp.minimum(a_re.astype(f32), -1e-4)
    lam_im = a_im.astype(f32)
    mag = jnp.exp(lam_re * dt)
    ab_re = mag * jnp.cos(lam_im * dt)
    ab_im = mag * jnp.sin(lam_im * dt)
    den = lam_re * lam_re + lam_im * lam_im
    nr = ab_re - 1.0
    q_re = (nr * lam_re + ab_im * lam_im) / den
    q_im = (ab_im * lam_re - nr * lam_im) / den
    br, bi = b_re.astype(f32), b_im.astype(f32)
    bb_re = q_re[..., None] * br - q_im[..., None] * bi
    bb_im = q_re[..., None] * bi + q_im[..., None] * br

    gpt = S5_KT // S5_GROUP
    eye = jnp.eye(gpt, dtype=bool)

    def b_blocks(bb):
        t = bb.reshape(S5_NT, gpt, S5_STATE, S5_GROUP)
        t = jnp.transpose(t, (0, 1, 3, 2))[:, :, :, None, :]
        blk = jnp.where(eye[None, :, None, :, None], t, 0.0)
        return blk.reshape(S5_NT, S5_KT, S5_ST).astype(bf16)

    def c_blocks(cc):
        t = cc.reshape(S5_NT, gpt, S5_GROUP, S5_STATE)
        t = jnp.transpose(t, (0, 1, 3, 2))[:, :, :, None, :]
        blk = jnp.where(eye[None, :, None, :, None], t, 0.0)
        return blk.reshape(S5_NT, S5_ST, S5_KT).astype(bf16)

    def csq(z):
        return z[0] * z[0] - z[1] * z[1], 2.0 * z[0] * z[1]

    def cmul(x, y):
        return x[0] * y[0] - x[1] * y[1], x[0] * y[1] + x[1] * y[0]

    a1 = (ab_re.reshape(-1), ab_im.reshape(-1))
    p = a1
    for _ in range(int(math.log2(S5_T))):
        p = csq(p)
    p1 = p
    p2 = csq(p1)
    p4 = csq(p2)
    pc = [(jnp.ones_like(p1[0]), jnp.zeros_like(p1[0]))]
    for _ in range(SUBLANE - 1):
        pc.append(cmul(pc[-1], p1))
    rows = [a1[0], a1[1], p1[0], p1[1], p2[0], p2[1], p4[0], p4[1]]
    rows += [z[0] for z in pc] + [z[1] for z in pc]
    tab = jnp.stack(rows, axis=0)
    return (b_blocks(bb_re), b_blocks(bb_im), c_blocks(c_re.astype(f32)), c_blocks(-c_im.astype(f32)), tab)


def _scan_perm():
    p = np.zeros((S5_TB, S5_TB), np.float32)
    for c in range(SUBLANE):
        for t in range(S5_T):
            p[t * SUBLANE + c, c * S5_T + t] = 1.0
    return p


def _perm_rows(dilation):
    return max(LANE, BF16_SUBLANES * dilation)


def _stride_perm(tile, dilation):
    p = np.zeros((tile, tile), np.float32)
    per = tile // dilation
    for j in range(per):
        for r in range(dilation):
            p[r * per + j, j * dilation + r] = 1.0
    return p


def _s5_layer(x, w_in, bre, bim, cre, cimn, tab, d_skip, w_glu, ln_g, ln_b, batch):
    n = x.shape[0]
    seq = n // batch
    row_spec = pl.BlockSpec((batch, S5_TB, D_MODEL), lambda t: (0, t, 0))
    perm = _scan_perm()
    out = pl.pallas_call(
        _s5_kernel,
        grid=(seq // S5_TB,),
        in_specs=[row_spec,
                  _const_spec((S5_TB, S5_TB)), _const_spec((S5_TB, S5_TB)),
                  _const_spec((D_MODEL, D_MODEL)),
                  _const_spec((S5_NT, S5_KT, S5_ST)), _const_spec((S5_NT, S5_KT, S5_ST)),
                  _const_spec((S5_NT, S5_ST, S5_KT)), _const_spec((S5_NT, S5_ST, S5_KT)),
                  _const_spec((_TAB_ROWS, S5_WIDTH)),
                  _const_spec((1, D_MODEL)),
                  _const_spec((D_MODEL, 2 * D_MODEL)),
                  _const_spec((1, D_MODEL)), _const_spec((1, D_MODEL))],
        out_specs=row_spec,
        out_shape=jax.ShapeDtypeStruct((batch, seq, D_MODEL), f32),
        scratch_shapes=[pltpu.VMEM((batch, S5_TB, D_MODEL), f32),
                        pltpu.VMEM((batch, S5_TB, S5_WIDTH), f32), pltpu.VMEM((batch, S5_TB, S5_WIDTH), f32),
                        pltpu.VMEM((batch, SUBLANE, S5_WIDTH), f32),
                        pltpu.VMEM((batch, SUBLANE, S5_WIDTH), f32)],
        compiler_params=_cparams(("arbitrary",)),
        name="s5_mixer",
    )(x.reshape(batch, seq, D_MODEL), jnp.asarray(perm, bf16), jnp.asarray(perm.T, bf16), w_in, bre, bim, cre,
      cimn, tab, d_skip, w_glu, ln_g, ln_b)
    return out.reshape(n, D_MODEL)


def _ffn_kernel(x_ref, w1_ref, w3_ref, w2_ref, g_ref, b_ref, *rest):
    perm_refs = rest[:N_DIL_GROUPS - 1]
    o_ref = rest[N_DIL_GROUPS - 1]
    xg_refs = rest[N_DIL_GROUPS:]
    x = x_ref[...]
    xb = x.astype(bf16)
    h = jax.nn.silu(_dot(xb, w1_ref[...])) * _dot(xb, w3_ref[...])
    f = _dot(h.astype(bf16), w2_ref[...])
    y = _layer_norm_rows(DEEPNORM_ALPHA * x + f, g_ref[...], b_ref[...])
    o_ref[...] = y
    yb = y.astype(bf16)
    xg_refs[0][...] = yb
    for perm_ref, xg_ref in zip(perm_refs, xg_refs[1:]):
        dil = xg_ref.shape[1]
        blk = perm_ref.shape[0]
        piece = blk // dil
        for s in range(FFN_TM // blk):
            v = _dot(perm_ref[...], yb[s * blk:(s + 1) * blk]).astype(bf16)
            xg_ref[0, :, s * piece:(s + 1) * piece, :] = v.reshape(dil, piece, D_MODEL)


def _strided_spec(tile, dilation, width, tiles_per_seq):
    return pl.BlockSpec((1, dilation, tile // dilation, width),
                        lambda i: (i // tiles_per_seq, 0, lax.rem(i, tiles_per_seq), 0))


def _ffn_layer(x, w1, w3, w2, ln_g, ln_b, batch):
    n = x.shape[0]
    seq = n // batch
    d_ff = w1.shape[1]
    row_spec = pl.BlockSpec((FFN_TM, D_MODEL), lambda i: (i, 0))
    dils = [dil for _, dil in DILATED_PAIRS[1:]]
    perms = [jnp.asarray(_stride_perm(_perm_rows(dil), dil), bf16) for dil in dils]
    return pl.pallas_call(
        _ffn_kernel,
        grid=(n // FFN_TM,),
        in_specs=[row_spec, _const_spec((D_MODEL, d_ff)), _const_spec((D_MODEL, d_ff)),
                  _const_spec((d_ff, D_MODEL)), _const_spec((1, D_MODEL)), _const_spec((1, D_MODEL))]
                 + [_const_spec(p.shape) for p in perms],
        out_specs=[row_spec, row_spec] + [_strided_spec(FFN_TM, dil, D_MODEL, seq // FFN_TM) for dil in dils],
        out_shape=[jax.ShapeDtypeStruct((n, D_MODEL), f32), jax.ShapeDtypeStruct((n, D_MODEL), bf16)]
                  + [jax.ShapeDtypeStruct((batch, dil, seq // dil, D_MODEL), bf16) for dil in dils],
        compiler_params=_cparams(("arbitrary",)),
        name="dense_ffn",
    )(x, w1, w3, w2, ln_g, ln_b, *perms)


def _qkv_kernel(x_ref, w_ref, o_ref):
    x = x_ref[...]
    for j in range(3):
        cs = slice(j * D_MODEL, (j + 1) * D_MODEL)
        r = _dot(x, w_ref[:, cs])
        if j == 0:
            r = r * (LOG2_E / math.sqrt(HEAD_DIM))
        o_ref[:, cs] = r.astype(o_ref.dtype)


def _qkv_proj(xg, w_qkv, group):
    n = xg.shape[0]
    return pl.pallas_call(
        _qkv_kernel,
        grid=(n // QKV_TM,),
        in_specs=[pl.BlockSpec((QKV_TM, D_MODEL), lambda i: (i, 0)),
                  pl.BlockSpec((D_MODEL, 3 * D_MODEL), lambda i: (0, group), pipeline_mode=pl.Buffered(1))],
        out_specs=pl.BlockSpec((QKV_TM, 3 * D_MODEL), lambda i: (i, 0)),
        out_shape=jax.ShapeDtypeStruct((n, 3 * D_MODEL), bf16),
        compiler_params=_cparams(("arbitrary",)),
        name="qkv_proj",
    )(xg, w_qkv)


def _t5_bucket(dist):
    max_exact = N_BUCKETS // 2
    d = np.maximum(dist, 0)
    large = max_exact + (np.log(np.maximum(d, 1) / max_exact) / math.log(MAX_DISTANCE / max_exact)
                         * (N_BUCKETS - max_exact)).astype(np.int64)
    large = np.minimum(large, N_BUCKETS - 1)
    return np.where(d < max_exact, d, large).astype(np.int32)


def _attn_bias(rel_bias):
    L = ATTN_BLOCK
    ki = np.arange(2 * L)[:, None]
    qi = np.arange(L)[None, :]
    off = qi + L - ki
    out = []
    for window, dilation in DILATED_PAIRS:
        band = (off >= 0) & (off <= window // dilation)
        bucket = _t5_bucket(np.clip(off, 0, None) * dilation)
        onehot = jnp.asarray(bucket[None] == np.arange(N_BUCKETS)[:, None, None], f32)
        bias = jnp.einsum('bkl,bh->hkl', onehot, rel_bias.astype(f32), precision=lax.Precision.HIGHEST)
        out.append(jnp.where(band[None], bias * LOG2_E, NEG_INF))
    return jnp.stack(out, axis=0)


def _attn_kernel(q_ref, kp_ref, kc_ref, vp_ref, vc_ref, bias_ref, o_ref, lse_ref, *, blocks_per_seq):
    L = ATTN_BLOCK
    low = lax.broadcasted_iota(jnp.int32, (L, LANE), 1) < HEAD_DIM
    head_row = lax.broadcasted_iota(jnp.int32, (N_HEADS, L), 0)
    is_first = lax.rem(pl.program_id(0) * ATT_QB, blocks_per_seq) == 0
    key_row = lax.broadcasted_iota(jnp.int32, (2 * L, L), 0)
    pen = jnp.where((key_row < L) & is_first, NEG_INF, 0.0).astype(f32)
    for s in range(ATT_QB):
        rs = slice(s * L, (s + 1) * L)
        ks = slice((s - 1) * L, (s + 1) * L)
        lse_t = jnp.zeros((N_HEADS, L), f32)
        for hp in range(N_HEADS // 2):
            cs = slice(hp * LANE, (hp + 1) * LANE)
            qp = q_ref[rs, cs]
            if s == 0:
                kpair = jnp.concatenate([kp_ref[:, cs], kc_ref[rs, cs]], axis=0)
                vpair = jnp.concatenate([vp_ref[:, cs], vc_ref[rs, cs]], axis=0)
            else:
                kpair, vpair = kc_ref[ks, cs], vc_ref[ks, cs]
            outs = []
            for half in range(2):
                h = 2 * hp + half
                qm = jnp.where(low if half == 0 else jnp.logical_not(low), qp, jnp.zeros_like(qp))
                logits = lax.dot_general(kpair, qm, (((1,), (1,)), ((), ())),
                                         preferred_element_type=f32)
                logits = logits + bias_ref[h]
                if s == 0:
                    logits = logits + pen
                m = jnp.max(logits, axis=0, keepdims=True)
                p = jnp.exp2(logits - m)
                ssum = jnp.sum(p, axis=0, keepdims=True)
                pn = (p * (1.0 / ssum)).astype(bf16)
                outs.append(lax.dot_general(pn, vpair, (((0,), (0,)), ((), ())),
                                            preferred_element_type=f32))
                lse_t = jnp.where(head_row == h, (m + jnp.log2(ssum)) * math.log(2.0), lse_t)
            o_ref[rs, cs] = jnp.where(low, outs[0], outs[1]).astype(o_ref.dtype)
        lse_rows = jnp.concatenate([lse_t, jnp.zeros((LANE - N_HEADS, L), f32)], axis=0)
        lse_ref[rs, :] = lse_rows.T


def _attention(qkv, bias, blocks_per_seq):
    n = qkv.shape[0]
    L = ATTN_BLOCK
    tq = ATT_QB * L
    assert blocks_per_seq % ATT_QB == 0

    def cur(col):
        return pl.BlockSpec((tq, D_MODEL), lambda i: (i, col))

    def prev(col):
        return pl.BlockSpec((L, D_MODEL), lambda i: (jnp.maximum(i * ATT_QB - 1, 0), col))

    return pl.pallas_call(
        functools.partial(_attn_kernel, blocks_per_seq=blocks_per_seq),
        grid=(n // tq,),
        in_specs=[cur(0), prev(1), cur(1), prev(2), cur(2), _const_spec((N_HEADS, 2 * L, L))],
        out_specs=[pl.BlockSpec((tq, D_MODEL), lambda i: (i, 0)), pl.BlockSpec((tq, LANE), lambda i: (i, 0))],
        out_shape=[jax.ShapeDtypeStruct((n, D_MODEL), bf16), jax.ShapeDtypeStruct((n, LANE), f32)],
        compiler_params=_cparams(("arbitrary",)),
        name="dilated_attn",
    )(qkv, qkv, qkv, qkv, qkv, bias)


def _split_bf16(v):
    hi = v.astype(bf16)
    return hi, (v - hi.astype(f32)).astype(bf16)


def _unstride_rows(ref, perm_ref, apply):
    dil, per, width = ref.shape[1:]
    blk = perm_ref.shape[0]
    piece = blk // dil
    outs = []
    for s in range(dil * per // blk):
        v = ref[0, :, s * piece:(s + 1) * piece, :].reshape(blk, width)
        outs.append(apply(perm_ref[...], v))
    return jnp.concatenate(outs, axis=0)


def _merge_kernel(*refs):
    ng = N_DIL_GROUPS
    o_refs, lse_refs, perm_refs = refs[:ng], refs[ng:2 * ng], refs[2 * ng:3 * ng - 1]
    (x_ref, e_ref, wo_ref, g_ref, b_ref, wrh_ref, wrl_ref, x3_ref, gate_ref, sel_ref) = refs[3 * ng - 1:]
    os_ = [o_refs[0][...].astype(f32)]
    ls = [lse_refs[0][...]]
    for o_ref, lse_ref, perm_ref in zip(o_refs[1:], lse_refs[1:], perm_refs):
        os_.append(_unstride_rows(o_ref, perm_ref, lambda p, v: _dot(p, v)))
        ls.append(_unstride_rows(lse_ref, perm_ref, _permute_rows_f32))
    mx = functools.reduce(jnp.maximum, ls)
    es = [jnp.exp(l - mx) for l in ls]
    den = functools.reduce(jnp.add, es)
    merged = None
    for g in range(ng):
        wfull = _dot((es[g] / den).astype(bf16), e_ref[...])
        term = wfull * os_[g]
        merged = term if merged is None else merged + term
    att = _dot(merged.astype(bf16), wo_ref[...])
    x3 = _layer_norm_rows(DEEPNORM_ALPHA * x_ref[...] + att, g_ref[...], b_ref[...])
    x3_ref[...] = x3

    xh, xl = _split_bf16(x3)
    logits = _dot(xh, wrh_ref[...]) + _dot(xl, wrh_ref[...]) + _dot(xh, wrl_ref[...])
    lane = lax.broadcasted_iota(jnp.int32, logits.shape, 1)
    neg = -jnp.inf
    logits = jnp.where(lane < N_EXPERTS, logits, neg)
    m1 = jnp.max(logits, axis=-1, keepdims=True)
    i1 = jnp.min(jnp.where(logits == m1, lane, LANE), axis=-1, keepdims=True)
    pick1 = lane == i1
    rest = jnp.where(pick1, neg, logits)
    m2 = jnp.max(rest, axis=-1, keepdims=True)
    i2 = jnp.min(jnp.where(rest == m2, lane, LANE), axis=-1, keepdims=True)
    pick2 = lane == i2
    e2 = jnp.exp(m2 - m1)
    gate_ref[...] = jnp.where(pick1, 1.0 / (1.0 + e2), jnp.where(pick2, e2 / (1.0 + e2), 0.0))
    sel_ref[...] = jnp.where(pick1 | pick2, 1.0, 0.0)


def _merge_layer(os_, lses, x, expand, w_o, ln_g, ln_b, wr_hi, wr_lo, batch):
    n = x.shape[0]
    tiles_per_seq = n // batch // MRG_TM
    row = lambda w: pl.BlockSpec((MRG_TM, w), lambda i: (i, 0))
    dils = [dil for _, dil in DILATED_PAIRS[1:]]
    perms = [jnp.asarray(_stride_perm(_perm_rows(dil), dil).T, bf16) for dil in dils]
    return pl.pallas_call(
        _merge_kernel,
        grid=(n // MRG_TM,),
        in_specs=[row(D_MODEL)] + [_strided_spec(MRG_TM, dil, D_MODEL, tiles_per_seq) for dil in dils]
                 + [row(LANE)] + [_strided_spec(MRG_TM, dil, LANE, tiles_per_seq) for dil in dils]
                 + [_const_spec(p.shape) for p in perms]
                 + [row(D_MODEL), _const_spec((LANE, D_MODEL)),
                    _const_spec((D_MODEL, D_MODEL)), _const_spec((1, D_MODEL)), _const_spec((1, D_MODEL)),
                    _const_spec((D_MODEL, LANE)), _const_spec((D_MODEL, LANE))],
        out_specs=[row(D_MODEL), row(LANE), row(LANE)],
        out_shape=[jax.ShapeDtypeStruct((n, D_MODEL), f32), jax.ShapeDtypeStruct((n, LANE), f32),
                   jax.ShapeDtypeStruct((n, LANE), f32)],
        compiler_params=_cparams(("arbitrary",)),
        name="attn_merge",
    )(*os_, *lses, *perms, x, expand, w_o, ln_g, ln_b, wr_hi, wr_lo)


def _row_copy(src_hbm, src_row, dst_ref, dst_row, sem):
    return pltpu.make_async_copy(src_hbm.at[pl.ds(src_row, 1), :], dst_ref.at[pl.ds(dst_row, 1), :], sem)


def _dispatch_kernel(pad_start_ref, pad_cnt_ref, ra_ref, rb_ref, x_ref, xs_hbm, zbuf, sem, zsem):
    def fill_copy(e, k):
        return _row_copy(zbuf, 0, xs_hbm, pad_start_ref[e] + k, zsem)

    def tail_copy(k):
        start = pl.multiple_of(pad_start_ref[N_EXPERTS] + k * MOE_TM, MOE_TM)
        return pltpu.make_async_copy(zbuf, xs_hbm.at[pl.ds(start, MOE_TM), :], zsem)

    @pl.when(pl.program_id(0) == 0)
    def _():
        zbuf[...] = jnp.zeros_like(zbuf)
        for e in range(N_EXPERTS):
            lax.fori_loop(0, pad_cnt_ref[e], lambda k, c: (fill_copy(e, k).start(), c)[1], 0)
        lax.fori_loop(0, pad_cnt_ref[N_EXPERTS], lambda k, c: (tail_copy(k).start(), c)[1], 0)

    def issue(r, carry):
        _row_copy(x_ref, r, xs_hbm, ra_ref[0, 0, r], sem).start()
        _row_copy(x_ref, r, xs_hbm, rb_ref[0, 0, r], sem).start()
        return carry

    lax.fori_loop(0, GATHER_TM, issue, 0, unroll=ISSUE_UNROLL)

    for _ in range(TOP_K):
        pltpu.make_async_copy(x_ref, xs_hbm.at[pl.ds(0, GATHER_TM), :], sem).wait()

    @pl.when(pl.program_id(0) == 0)
    def _():
        for e in range(N_EXPERTS):
            lax.fori_loop(0, pad_cnt_ref[e], lambda k, c: (fill_copy(e, k).wait(), c)[1], 0)
        lax.fori_loop(0, pad_cnt_ref[N_EXPERTS], lambda k, c: (tail_copy(k).wait(), c)[1], 0)


def _dispatch_rows(pad_start, pad_cnt, ra, rb, x, rows_total):
    n = x.shape[0]
    smem = pl.BlockSpec((1, 1, GATHER_TM), lambda i, ps, pc: (i, 0, 0), memory_space=pltpu.SMEM)
    grid_spec = pltpu.PrefetchScalarGridSpec(
        num_scalar_prefetch=2,
        grid=(n // GATHER_TM,),
        in_specs=[smem, smem, pl.BlockSpec((GATHER_TM, D_MODEL), lambda i, ps, pc: (i, 0))],
        out_specs=pl.BlockSpec(memory_space=pl.ANY),
        scratch_shapes=[pltpu.VMEM((MOE_TM, D_MODEL), x.dtype), pltpu.SemaphoreType.DMA(()),
                        pltpu.SemaphoreType.DMA(())],
    )
    return pl.pallas_call(
        _dispatch_kernel,
        grid_spec=grid_spec,
        out_shape=jax.ShapeDtypeStruct((rows_total, D_MODEL), x.dtype),
        compiler_params=_cparams(("arbitrary",)),
        name="moe_dispatch",
    )(pad_start, pad_cnt, ra, rb, x)


def _moe_kernel(te_ref, act_ref, x_ref, w1_ref, w3_ref, w2_ref, o_ref):
    del te_ref
    i, f = pl.program_id(0), pl.program_id(1)

    @pl.when(act_ref[i] == 1)
    def _():
        xb = x_ref[...].astype(bf16)
        h = jax.nn.silu(_dot(xb, w1_ref[0])) * _dot(xb, w3_ref[0])
        part = _dot(h.astype(bf16), w2_ref[0])

        @pl.when(f == 0)
        def _():
            o_ref[...] = part

        @pl.when(f > 0)
        def _():
            o_ref[...] = o_ref[...] + part

    @pl.when((act_ref[i] == 0) & (f == 0))
    def _():
        o_ref[...] = jnp.zeros_like(o_ref)


def _moe_gmm(tile_expert, tile_active, xs, w1, w3, w2):
    n_tiles = tile_expert.shape[0]
    d_ff = w1.shape[2]
    grid_spec = pltpu.PrefetchScalarGridSpec(
        num_scalar_prefetch=2,
        grid=(n_tiles, d_ff // MOE_TF),
        in_specs=[pl.BlockSpec((MOE_TM, D_MODEL), lambda i, f, te, act: (i * act[i], 0)),
                  pl.BlockSpec((1, D_MODEL, MOE_TF), lambda i, f, te, act: (te[i], 0, f)),
                  pl.BlockSpec((1, D_MODEL, MOE_TF), lambda i, f, te, act: (te[i], 0, f)),
                  pl.BlockSpec((1, MOE_TF, D_MODEL), lambda i, f, te, act: (te[i], f, 0))],
        out_specs=pl.BlockSpec((MOE_TM, D_MODEL), lambda i, f, te, act: (i, 0)),
    )
    return pl.pallas_call(
        _moe_kernel,
        grid_spec=grid_spec,
        out_shape=jax.ShapeDtypeStruct((n_tiles * MOE_TM, D_MODEL), f32),
        compiler_params=_cparams(("arbitrary", "arbitrary")),
        name="moe_gmm",
    )(tile_expert, tile_active, xs, w1, w3, w2)


def _combine_kernel(ra_ref, rb_ref, ra_next, rb_next, ga_ref, gb_ref, x_ref, y_hbm, g_ref, b_ref, o_ref,
                    ybuf, sem):
    i = pl.program_id(0)
    slot = lax.rem(i, 2)

    def gather(ia_ref, ib_ref, s):
        def issue(r, carry):
            _row_copy(y_hbm, ia_ref[0, 0, r], ybuf.at[s, 0], r, sem.at[s, 0]).start()
            _row_copy(y_hbm, ib_ref[0, 0, r], ybuf.at[s, 1], r, sem.at[s, 1]).start()
            return carry

        lax.fori_loop(0, COMBINE_TM, issue, 0, unroll=ISSUE_UNROLL)

    @pl.when(i == 0)
    def _():
        gather(ra_ref, rb_ref, 0)

    @pl.when(i + 1 < pl.num_programs(0))
    def _():
        gather(ra_next, rb_next, 1 - slot)

    for k in range(TOP_K):
        pltpu.make_async_copy(y_hbm.at[pl.ds(0, COMBINE_TM), :], ybuf.at[slot, k], sem.at[slot, k]).wait()
    f = ga_ref[...] * ybuf[slot, 0] + gb_ref[...] * ybuf[slot, 1]
    o_ref[...] = _layer_norm_rows(DEEPNORM_ALPHA * x_ref[...] + f, g_ref[...], b_ref[...])


def _combine(ra, rb, ga, gb, x, ys, ln_g, ln_b):
    n = x.shape[0]
    last = n // COMBINE_TM - 1
    smem = pl.BlockSpec((1, 1, COMBINE_TM), lambda i: (i, 0, 0), memory_space=pltpu.SMEM)
    smem_next = pl.BlockSpec((1, 1, COMBINE_TM), lambda i: (jnp.minimum(i + 1, last), 0, 0),
                             memory_space=pltpu.SMEM)
    col = pl.BlockSpec((COMBINE_TM, 1), lambda i: (i, 0))
    row = pl.BlockSpec((COMBINE_TM, D_MODEL), lambda i: (i, 0))
    return pl.pallas_call(
        _combine_kernel,
        grid=(n // COMBINE_TM,),
        in_specs=[smem, smem, smem_next, smem_next, col, col, row, pl.BlockSpec(memory_space=pl.ANY),
                  _const_spec((1, D_MODEL)), _const_spec((1, D_MODEL))],
        out_specs=row,
        out_shape=jax.ShapeDtypeStruct((n, D_MODEL), f32),
        scratch_shapes=[pltpu.VMEM((2, TOP_K, COMBINE_TM, D_MODEL), f32), pltpu.SemaphoreType.DMA((2, TOP_K))],
        compiler_params=_cparams(("arbitrary",)),
        name="moe_combine",
    )(ra, rb, ra, rb, ga, gb, x, ys, ln_g, ln_b)


def _moe_plan(sel, gates):
    n = sel.shape[0]
    rows_total = TOP_K * n + N_EXPERTS * MOE_TM
    n_tiles = rows_total // MOE_TM
    seli = sel.astype(jnp.int32)
    cnt = jnp.sum(seli, axis=0)
    pos = jnp.cumsum(seli, axis=0) - seli
    padded = ((cnt + MOE_TM - 1) // MOE_TM) * MOE_TM
    bounds = jnp.cumsum(padded)
    row = (bounds - padded)[None, :] + pos
    pad_start = jnp.concatenate([bounds - padded + cnt, bounds[-1:]]).astype(jnp.int32)
    pad_cnt = jnp.concatenate([padded - cnt, (rows_total - bounds[-1:]) // MOE_TM]).astype(jnp.int32)
    tile_start = jnp.arange(n_tiles, dtype=jnp.int32) * MOE_TM
    last_used = jnp.max(jnp.where(cnt > 0, jnp.arange(N_EXPERTS, dtype=jnp.int32), 0))
    tile_expert = jnp.sum((tile_start[:, None] >= bounds[None, :]).astype(jnp.int32), axis=1)
    tile_expert = jnp.minimum(tile_expert, last_used)
    tile_active = (tile_start < bounds[-1]).astype(jnp.int32)
    eid = jnp.arange(N_EXPERTS, dtype=jnp.int32)[None, :]
    is_a = eid == jnp.min(jnp.where(sel, eid, N_EXPERTS), axis=1, keepdims=True)
    is_b = eid == jnp.max(jnp.where(sel, eid, -1), axis=1, keepdims=True)
    pick = lambda a, m: jnp.sum(jnp.where(m, a, 0), axis=1)
    return (pad_start, pad_cnt, tile_expert, tile_active,
            pick(row, is_a).astype(jnp.int32), pick(row, is_b).astype(jnp.int32),
            pick(gates, is_a)[:, None], pick(gates, is_b)[:, None])


def kernel(x, s5_w_in, s5_a_re, s5_a_im, s5_log_dt, s5_b_re, s5_b_im, s5_c_re, s5_c_im, s5_d, s5_w_glu,
           attn_w_qkv, attn_w_o, rel_bias, ffn_w1, ffn_w3, ffn_w2, moe_w_router, moe_w1, moe_w3, moe_w2,
           ln_g, ln_b):
    batch, seq, d = x.shape
    n = batch * seq
    assert d == D_MODEL
    assert seq % (DILATED_PAIRS[-1][1] * ATTN_BLOCK * ATT_QB) == 0
    assert seq % S5_TB == 0 and seq % FFN_TM == 0 and FFN_TM == MRG_TM
    vec = lambda v: v.reshape(1, D_MODEL).astype(f32)

    bre, bim, cre, cimn, tab = _s5_tables(s5_a_re[0], s5_a_im[0], s5_log_dt[0], s5_b_re[0], s5_b_im[0],
                                          s5_c_re[0], s5_c_im[0])
    x1 = _s5_layer(x.reshape(n, d), s5_w_in[0].astype(bf16), bre, bim, cre, cimn, tab, vec(s5_d[0]),
                   s5_w_glu[0].astype(bf16), vec(ln_g[0, 0]), vec(ln_b[0, 0]), batch)
    x2, *xgs = _ffn_layer(x1, ffn_w1[0].astype(bf16), ffn_w3[0].astype(bf16), ffn_w2[0].astype(bf16),
                          vec(ln_g[0, 1]), vec(ln_b[0, 1]), batch)

    wq = attn_w_qkv[0].astype(bf16)
    bias = _attn_bias(rel_bias)
    os_, lses = [], []
    for g, (_, dil) in enumerate(DILATED_PAIRS):
        qkv = _qkv_proj(xgs[g].reshape(n, d), wq, g)
        o, lse = _attention(qkv, bias[g], seq // (dil * ATTN_BLOCK))
        os_.append(o if g == 0 else o.reshape(batch, dil, seq // dil, d))
        lses.append(lse if g == 0 else lse.reshape(batch, dil, seq // dil, LANE))
    expand = (jnp.arange(LANE)[:, None] == (jnp.arange(D_MODEL)[None, :] // HEAD_DIM)).astype(bf16)
    wr = jnp.pad(moe_w_router[0].astype(f32), ((0, 0), (0, LANE - N_EXPERTS)))
    wr_hi, wr_lo = _split_bf16(wr)
    x3, gates, sel = _merge_layer(os_, lses, x2, expand, attn_w_o[0].astype(bf16), vec(ln_g[1, 0]),
                                  vec(ln_b[1, 0]), wr_hi, wr_lo, batch)

    pad_start, pad_cnt, tile_expert, tile_active, ra, rb, ga, gb = _moe_plan(sel[:, :N_EXPERTS] > 0.5,
                                                                             gates[:, :N_EXPERTS])
    xs = _dispatch_rows(pad_start, pad_cnt, ra.reshape(-1, 1, GATHER_TM), rb.reshape(-1, 1, GATHER_TM), x3,
                        TOP_K * n + N_EXPERTS * MOE_TM)
    ys = _moe_gmm(tile_expert, tile_active, xs, moe_w1[0].astype(bf16), moe_w3[0].astype(bf16),
                  moe_w2[0].astype(bf16))
    out = _combine(ra.reshape(-1, 1, COMBINE_TM), rb.reshape(-1, 1, COMBINE_TM), ga, gb, x3, ys,
                   vec(ln_g[1, 1]), vec(ln_b[1, 1]))
    return out.reshape(batch, seq, d)
```

```python
import functools
import math

import numpy as np
import jax
import jax.numpy as jnp
from jax import lax
from jax.experimental import pallas as pl
from jax.experimental.pallas import tpu as pltpu

D_MODEL = 1024
S5_GROUP = 16
S5_GROUPS = D_MODEL // S5_GROUP
S5_STATE = 64
S5_WIDTH = S5_GROUPS * S5_STATE
HEAD_DIM = 64
N_HEADS = D_MODEL // HEAD_DIM
DILATED_PAIRS = ((128, 1), (512, 4), (2048, 16))
N_DIL_GROUPS = len(DILATED_PAIRS)
ATTN_BLOCK = 128
N_BUCKETS = 32
MAX_DISTANCE = 2048
N_EXPERTS = 8
TOP_K = 2
DEPTH = 2
DEEPNORM_ALPHA = (2 * DEPTH) ** 0.25
LN_EPS = 1e-5
NEG_INF = -1e30
LOG2_E = math.log2(math.e)

LANE = 128
SUBLANE = 8
BF16_SUBLANES = 16
MXU_DIM = 256
VMEM_LIMIT = 56 * 1024 * 1024

S5_T = 32
S5_TB = SUBLANE * S5_T
S5_LC = 512
S5_KT = MXU_DIM
S5_NT = D_MODEL // S5_KT
S5_ST = S5_WIDTH // S5_NT
FFN_TM = 512
QKV_TM = 1024
ATT_QB = 8
MRG_TM = 512
MOE_TM = 512
MOE_TF = 1792
GATHER_TM = 1024
COMBINE_TM = 512
ISSUE_UNROLL = 16

f32 = jnp.float32
bf16 = jnp.bfloat16


def _cparams(sem):
    return pltpu.CompilerParams(dimension_semantics=sem, vmem_limit_bytes=VMEM_LIMIT)


def _const_spec(shape):
    nd = len(shape)
    return pl.BlockSpec(shape, lambda *_: (0,) * nd, pipeline_mode=pl.Buffered(1))


def _layer_norm_rows(v, g, b):
    mu = jnp.mean(v, axis=-1, keepdims=True)
    c = v - mu
    var = jnp.mean(c * c, axis=-1, keepdims=True)
    return c * lax.rsqrt(var + LN_EPS) * g + b


def _gelu_tanh(x):
    return 0.5 * x * (1.0 + jnp.tanh(math.sqrt(2.0 / math.pi) * (x + 0.044715 * (x * x * x))))


def _dot(a, b):
    return jnp.dot(a, b, preferred_element_type=f32)


_TAB_A = 0
_TAB_P1 = 2
_TAB_P2 = 4
_TAB_P4 = 6
_TAB_PC_RE = 8
_TAB_PC_IM = 16
_TAB_ROWS = 24


def _split3_bf16(v):
    hi = v.astype(bf16)
    r1 = v - hi.astype(f32)
    mid = r1.astype(bf16)
    return hi, mid, (r1 - mid.astype(f32)).astype(bf16)


def _permute_rows_f32(perm, v):
    hi, mid, lo = _split3_bf16(v)
    return _dot(perm, hi) + _dot(perm, mid) + _dot(perm, lo)


def _s5_kernel(x_ref, pf_ref, pb_ref, win_ref, bre_ref, bim_ref, cre_ref, cimn_ref, tab_ref, d_ref, wglu_ref,
               g_ref, b_ref, o_ref, u_ref, hre_ref, him_ref, car_re, car_im):
    @pl.when(pl.program_id(0) == 0)
    def _():
        car_re[...] = jnp.zeros_like(car_re)
        car_im[...] = jnp.zeros_like(car_im)

    nb = x_ref.shape[0]
    phases = [_s5_tile(x_ref[b], functools.partial(o_ref.__setitem__, b), pf_ref, pb_ref, win_ref, bre_ref,
                       bim_ref, cre_ref, cimn_ref, tab_ref, d_ref, wglu_ref, g_ref, b_ref, u_ref.at[b],
                       hre_ref.at[b], him_ref.at[b], car_re.at[b], car_im.at[b]) for b in range(nb)]
    _interleave(phases[0][0], [])
    for b in range(nb):
        mxu_work = (phases[b - 1][2] if b > 0 else []) + (phases[b + 1][0] if b + 1 < nb else [])
        _interleave(phases[b][1], mxu_work)
    _interleave(phases[nb - 1][2], [])


def _interleave(a, b):
    done = 0
    for i, piece in enumerate(a):
        piece()
        while done < (i + 1) * len(b) // len(a):
            b[done]()
            done += 1


def _s5_tile(x, emit, pf_ref, pb_ref, win_ref, bre_ref, bim_ref, cre_ref, cimn_ref, tab_ref, d_ref, wglu_ref,
             g_ref, b_ref, u_ref, hre_ref, him_ref, car_re, car_im):
    T = S5_T

    def project_u():
        xs = _dot(pf_ref[...], x.astype(bf16)).astype(bf16)
        u_ref[...] = _dot(xs, win_ref[...])

    def project_b(kt):
        ub = u_ref[:, kt * S5_KT:(kt + 1) * S5_KT].astype(bf16)
        hre_ref[:, kt * S5_ST:(kt + 1) * S5_ST] = _dot(ub, bre_ref[kt])
        him_ref[:, kt * S5_ST:(kt + 1) * S5_ST] = _dot(ub, bim_ref[kt])

    def bcast(r, sl):
        return jnp.broadcast_to(tab_ref[r:r + 1, sl], (SUBLANE, S5_LC))

    def cmul(ar, ai, xr, xi):
        return ar * xr - ai * xi, ar * xi + ai * xr

    def recur(lc):
        row = lax.broadcasted_iota(jnp.int32, (SUBLANE, S5_LC), 0)
        sl = slice(lc * S5_LC, (lc + 1) * S5_LC)
        ar, ai = bcast(_TAB_A, sl), bcast(_TAB_A + 1, sl)

        er, ei = hre_ref[0:SUBLANE, sl], him_ref[0:SUBLANE, sl]
        for t in range(1, T):
            rows = slice(t * SUBLANE, (t + 1) * SUBLANE)
            pr, pi = cmul(ar, ai, er, ei)
            er = pr + hre_ref[rows, sl]
            ei = pi + him_ref[rows, sl]
            hre_ref[rows, sl] = er
            him_ref[rows, sl] = ei

        xr, xi = er, ei
        for s, tr in ((1, _TAB_P1), (2, _TAB_P2), (4, _TAB_P4)):
            sr = jnp.where(row >= s, pltpu.roll(xr, s, 0), 0.0)
            si = jnp.where(row >= s, pltpu.roll(xi, s, 0), 0.0)
            mr, mi = cmul(bcast(tr, sl), bcast(tr + 1, sl), sr, si)
            xr, xi = xr + mr, xi + mi
        shr = jnp.where(row >= 1, pltpu.roll(xr, 1, 0), 0.0)
        shi = jnp.where(row >= 1, pltpu.roll(xi, 1, 0), 0.0)
        cr, ci = car_re[:, sl], car_im[:, sl]
        pcr = tab_ref[_TAB_PC_RE:_TAB_PC_RE + SUBLANE, sl]
        pci = tab_ref[_TAB_PC_IM:_TAB_PC_IM + SUBLANE, sl]
        hr0, hi0 = cmul(pcr, pci, cr, ci)
        hr0, hi0 = hr0 + shr, hi0 + shi
        nr, ni = cmul(bcast(_TAB_P1, sl), bcast(_TAB_P1 + 1, sl), hr0, hi0)
        nr, ni = nr + er, ni + ei
        car_re[:, sl] = jnp.broadcast_to(nr[SUBLANE - 1:SUBLANE], (SUBLANE, S5_LC))
        car_im[:, sl] = jnp.broadcast_to(ni[SUBLANE - 1:SUBLANE], (SUBLANE, S5_LC))

        zr, zi = hr0, hi0
        for t in range(T):
            rows = slice(t * SUBLANE, (t + 1) * SUBLANE)
            zr, zi = cmul(ar, ai, zr, zi)
            hre_ref[rows, sl] = hre_ref[rows, sl] + zr
            him_ref[rows, sl] = him_ref[rows, sl] + zi

    gs, hs = [], []

    def project_c(nt):
        hr = hre_ref[:, nt * S5_ST:(nt + 1) * S5_ST].astype(bf16)
        hi = him_ref[:, nt * S5_ST:(nt + 1) * S5_ST].astype(bf16)
        csl = slice(nt * S5_KT, (nt + 1) * S5_KT)
        y = _dot(hr, cre_ref[nt]) + _dot(hi, cimn_ref[nt]) + d_ref[:, csl] * u_ref[:, csl]
        gs.append(_gelu_tanh(y).astype(bf16))

    def glu(j):
        g = jnp.concatenate(gs, axis=1)
        val = _dot(g, wglu_ref[:, j * S5_KT:(j + 1) * S5_KT])
        gate = _dot(g, wglu_ref[:, D_MODEL + j * S5_KT:D_MODEL + (j + 1) * S5_KT])
        hs.append(val * jax.nn.sigmoid(gate))

    def finish():
        h = _permute_rows_f32(pb_ref[...], jnp.concatenate(hs, axis=1))
        emit(_layer_norm_rows(DEEPNORM_ALPHA * x + h, g_ref[...], b_ref[...]))

    part = functools.partial
    return ([project_u] + [part(project_b, kt) for kt in range(S5_NT)],
            [part(recur, lc) for lc in range(S5_WIDTH // S5_LC)],
            [part(project_c, nt) for nt in range(S5_NT)] + [part(glu, j) for j in range(S5_NT)] + [finish])


def _s5_tables(a_re, a_im, log_dt, b_re, b_im, c_re, c_im):
    dt = jnp.exp(log_dt.astype(f32))[:, None]
    lam_re = jnp.minimum(a_re.astype(f32), -1e-4)
    lam_im = a_im.astype(f32)
    mag = jnp.exp(lam_re * dt)
    ab_re = mag * jnp.cos(lam_im * dt)
    ab_im = mag * jnp.sin(lam_im * dt)
    den = lam_re * lam_re + lam_im * lam_im
    nr = ab_re - 1.0
    q_re = (nr * lam_re + ab_im * lam_im) / den
    q_im = (ab_im * lam_re - nr * lam_im) / den
    br, bi = b_re.astype(f32), b_im.astype(f32)
    bb_re = q_re[..., None] * br - q_im[..., None] * bi
    bb_im = q_re[..., None] * bi + q_im[..., None] * br

    gpt = S5_KT // S5_GROUP
    eye = jnp.eye(gpt, dtype=bool)

    def b_blocks(bb):
        t = bb.reshape(S5_NT, gpt, S5_STATE, S5_GROUP)
        t = jnp.transpose(t, (0, 1, 3, 2))[:, :, :, None, :]
        blk = jnp.where(eye[None, :, None, :, None], t, 0.0)
        return blk.reshape(S5_NT, S5_KT, S5_ST).astype(bf16)

    def c_blocks(cc):
        t = cc.reshape(S5_NT, gpt, S5_GROUP, S5_STATE)
        t = jnp.transpose(t, (0, 1, 3, 2))[:, :, :, None, :]
        blk = jnp.where(eye[None, :, None, :, None], t, 0.0)
        return blk.reshape(S5_NT, S5_ST, S5_KT).astype(bf16)

    def csq(z):
        return z[0] * z[0] - z[1] * z[1], 2.0 * z[0] * z[1]

    def cmul(x, y):
        return x[0] * y[0] - x[1] * y[1], x[0] * y[1] + x[1] * y[0]

    a1 = (ab_re.reshape(-1), ab_im.reshape(-1))
    p = a1
    for _ in range(int(math.log2(S5_T))):
        p = csq(p)
    p1 = p
    p2 = csq(p1)
    p4 = csq(p2)
    pc = [(jnp.ones_like(p1[0]), jnp.zeros_like(p1[0]))]
    for _ in range(SUBLANE - 1):
        pc.append(cmul(pc[-1], p1))
    rows = [a1[0], a1[1], p1[0], p1[1], p2[0], p2[1], p4[0], p4[1]]
    rows += [z[0] for z in pc] + [z[1] for z in pc]
    tab = jnp.stack(rows, axis=0)
    return (b_blocks(bb_re), b_blocks(bb_im), c_blocks(c_re.astype(f32)), c_blocks(-c_im.astype(f32)), tab)


def _scan_perm():
    p = np.zeros((S5_TB, S5_TB), np.float32)
    for c in range(SUBLANE):
        for t in range(S5_T):
            p[t * SUBLANE + c, c * S5_T + t] = 1.0
    return p


def _perm_rows(dilation):
    return max(LANE, BF16_SUBLANES * dilation)


def _stride_perm(tile, dilation):
    p = np.zeros((tile, tile), np.float32)
    per = tile // dilation
    for j in range(per):
        for r in range(dilation):
            p[r * per + j, j * dilation + r] = 1.0
    return p


def _s5_layer(x, w_in, bre, bim, cre, cimn, tab, d_skip, w_glu, ln_g, ln_b, batch):
    n = x.shape[0]
    seq = n // batch
    row_spec = pl.BlockSpec((batch, S5_TB, D_MODEL), lambda t: (0, t, 0))
    perm = _scan_perm()
    out = pl.pallas_call(
        _s5_kernel,
        grid=(seq // S5_TB,),
        in_specs=[row_spec,
                  _const_spec((S5_TB, S5_TB)), _const_spec((S5_TB, S5_TB)),
                  _const_spec((D_MODEL, D_MODEL)),
                  _const_spec((S5_NT, S5_KT, S5_ST)), _const_spec((S5_NT, S5_KT, S5_ST)),
                  _const_spec((S5_NT, S5_ST, S5_KT)), _const_spec((S5_NT, S5_ST, S5_KT)),
                  _const_spec((_TAB_ROWS, S5_WIDTH)),
                  _const_spec((1, D_MODEL)),
                  _const_spec((D_MODEL, 2 * D_MODEL)),
                  _const_spec((1, D_MODEL)), _const_spec((1, D_MODEL))],
        out_specs=row_spec,
        out_shape=jax.ShapeDtypeStruct((batch, seq, D_MODEL), f32),
        scratch_shapes=[pltpu.VMEM((batch, S5_TB, D_MODEL), f32),
                        pltpu.VMEM((batch, S5_TB, S5_WIDTH), f32), pltpu.VMEM((batch, S5_TB, S5_WIDTH), f32),
                        pltpu.VMEM((batch, SUBLANE, S5_WIDTH), f32),
                        pltpu.VMEM((batch, SUBLANE, S5_WIDTH), f32)],
        compiler_params=_cparams(("arbitrary",)),
        name="s5_mixer",
    )(x.reshape(batch, seq, D_MODEL), jnp.asarray(perm, bf16), jnp.asarray(perm.T, bf16), w_in, bre, bim, cre,
      cimn, tab, d_skip, w_glu, ln_g, ln_b)
    return out.reshape(n, D_MODEL)


def _ffn_kernel(x_ref, w1_ref, w3_ref, w2_ref, g_ref, b_ref, *rest):
    perm_refs = rest[:N_DIL_GROUPS - 1]
    o_ref = rest[N_DIL_GROUPS - 1]
    xg_refs = rest[N_DIL_GROUPS:]
    x = x_ref[...]
    xb = x.astype(bf16)
    h = jax.nn.silu(_dot(xb, w1_ref[...])) * _dot(xb, w3_ref[...])
    f = _dot(h.astype(bf16), w2_ref[...])
    y = _layer_norm_rows(DEEPNORM_ALPHA * x + f, g_ref[...], b_ref[...])
    o_ref[...] = y
    yb = y.astype(bf16)
    xg_refs[0][...] = yb
    for perm_ref, xg_ref in zip(perm_refs, xg_refs[1:]):
        dil = xg_ref.shape[1]
        blk = perm_ref.shape[0]
        piece = blk // dil
        for s in range(FFN_TM // blk):
            v = _dot(perm_ref[...], yb[s * blk:(s + 1) * blk]).astype(bf16)
            xg_ref[0, :, s * piece:(s + 1) * piece, :] = v.reshape(dil, piece, D_MODEL)


def _strided_spec(tile, dilation, width, tiles_per_seq):
    return pl.BlockSpec((1, dilation, tile // dilation, width),
                        lambda i: (i // tiles_per_seq, 0, lax.rem(i, tiles_per_seq), 0))


def _ffn_layer(x, w1, w3, w2, ln_g, ln_b, batch):
    n = x.shape[0]
    seq = n // batch
    d_ff = w1.shape[1]
    row_spec = pl.BlockSpec((FFN_TM, D_MODEL), lambda i: (i, 0))
    dils = [dil for _, dil in DILATED_PAIRS[1:]]
    perms = [jnp.asarray(_stride_perm(_perm_rows(dil), dil), bf16) for dil in dils]
    return pl.pallas_call(
        _ffn_kernel,
        grid=(n // FFN_TM,),
        in_specs=[row_spec, _const_spec((D_MODEL, d_ff)), _const_spec((D_MODEL, d_ff)),
                  _const_spec((d_ff, D_MODEL)), _const_spec((1, D_MODEL)), _const_spec((1, D_MODEL))]
                 + [_const_spec(p.shape) for p in perms],
        out_specs=[row_spec, row_spec] + [_strided_spec(FFN_TM, dil, D_MODEL, seq // FFN_TM) for dil in dils],
        out_shape=[jax.ShapeDtypeStruct((n, D_MODEL), f32), jax.ShapeDtypeStruct((n, D_MODEL), bf16)]
                  + [jax.ShapeDtypeStruct((batch, dil, seq // dil, D_MODEL), bf16) for dil in dils],
        compiler_params=_cparams(("arbitrary",)),
        name="dense_ffn",
    )(x, w1, w3, w2, ln_g, ln_b, *perms)


def _qkv_kernel(x_ref, w_ref, o_ref):
    x = x_ref[...]
    for j in range(3):
        cs = slice(j * D_MODEL, (j + 1) * D_MODEL)
        r = _dot(x, w_ref[:, cs])
        if j == 0:
            r = r * (LOG2_E / math.sqrt(HEAD_DIM))
        o_ref[:, cs] = r.astype(o_ref.dtype)


def _qkv_proj(xg, w_qkv, group):
    n = xg.shape[0]
    return pl.pallas_call(
        _qkv_kernel,
        grid=(n // QKV_TM,),
        in_specs=[pl.BlockSpec((QKV_TM, D_MODEL), lambda i: (i, 0)),
                  pl.BlockSpec((D_MODEL, 3 * D_MODEL), lambda i: (0, group), pipeline_mode=pl.Buffered(1))],
        out_specs=pl.BlockSpec((QKV_TM, 3 * D_MODEL), lambda i: (i, 0)),
        out_shape=jax.ShapeDtypeStruct((n, 3 * D_MODEL), bf16),
        compiler_params=_cparams(("arbitrary",)),
        name="qkv_proj",
    )(xg, w_qkv)


def _t5_bucket(dist):
    max_exact = N_BUCKETS // 2
    d = np.maximum(dist, 0)
    large = max_exact + (np.log(np.maximum(d, 1) / max_exact) / math.log(MAX_DISTANCE / max_exact)
                         * (N_BUCKETS - max_exact)).astype(np.int64)
    large = np.minimum(large, N_BUCKETS - 1)
    return np.where(d < max_exact, d, large).astype(np.int32)


def _attn_bias(rel_bias):
    L = ATTN_BLOCK
    ki = np.arange(2 * L)[:, None]
    qi = np.arange(L)[None, :]
    off = qi + L - ki
    out = []
    for window, dilation in DILATED_PAIRS:
        band = (off >= 0) & (off <= window // dilation)
        bucket = _t5_bucket(np.clip(off, 0, None) * dilation)
        onehot = jnp.asarray(bucket[None] == np.arange(N_BUCKETS)[:, None, None], f32)
        bias = jnp.einsum('bkl,bh->hkl', onehot, rel_bias.astype(f32), precision=lax.Precision.HIGHEST)
        out.append(jnp.where(band[None], bias * LOG2_E, NEG_INF))
    return jnp.stack(out, axis=0)


def _attn_kernel(q_ref, kp_ref, kc_ref, vp_ref, vc_ref, bias_ref, o_ref, lse_ref, *, blocks_per_seq):
    L = ATTN_BLOCK
    low = lax.broadcasted_iota(jnp.int32, (L, LANE), 1) < HEAD_DIM
    head_row = lax.broadcasted_iota(jnp.int32, (N_HEADS, L), 0)
    is_first = lax.rem(pl.program_id(0) * ATT_QB, blocks_per_seq) == 0
    key_row = lax.broadcasted_iota(jnp.int32, (2 * L, L), 0)
    pen = jnp.where((key_row < L) & is_first, NEG_INF, 0.0).astype(f32)
    for s in range(ATT_QB):
        rs = slice(s * L, (s + 1) * L)
        ks = slice((s - 1) * L, (s + 1) * L)
        lse_t = jnp.zeros((N_HEADS, L), f32)
        for hp in range(N_HEADS // 2):
            cs = slice(hp * LANE, (hp + 1) * LANE)
            qp = q_ref[rs, cs]
            if s == 0:
                kpair = jnp.concatenate([kp_ref[:, cs], kc_ref[rs, cs]], axis=0)
                vpair = jnp.concatenate([vp_ref[:, cs], vc_ref[rs, cs]], axis=0)
            else:
                kpair, vpair = kc_ref[ks, cs], vc_ref[ks, cs]
            outs = []
            for half in range(2):
                h = 2 * hp + half
                qm = jnp.where(low if half == 0 else jnp.logical_not(low), qp, jnp.zeros_like(qp))
                logits = lax.dot_general(kpair, qm, (((1,), (1,)), ((), ())),
                                         preferred_element_type=f32)
                logits = logits + bias_ref[h]
                if s == 0:
                    logits = logits + pen
                m = jnp.max(logits, axis=0, keepdims=True)
                p = jnp.exp2(logits - m)
                ssum = jnp.sum(p, axis=0, keepdims=True)
                pn = (p * (1.0 / ssum)).astype(bf16)
                outs.append(lax.dot_general(pn, vpair, (((0,), (0,)), ((), ())),
                                            preferred_element_type=f32))
                lse_t = jnp.where(head_row == h, (m + jnp.log2(ssum)) * math.log(2.0), lse_t)
            o_ref[rs, cs] = jnp.where(low, outs[0], outs[1]).astype(o_ref.dtype)
        lse_rows = jnp.concatenate([lse_t, jnp.zeros((LANE - N_HEADS, L), f32)], axis=0)
        lse_ref[rs, :] = lse_rows.T


def _attention(qkv, bias, blocks_per_seq):
    n = qkv.shape[0]
    L = ATTN_BLOCK
    tq = ATT_QB * L
    assert blocks_per_seq % ATT_QB == 0

    def cur(col):
        return pl.BlockSpec((tq, D_MODEL), lambda i: (i, col))

    def prev(col):
        return pl.BlockSpec((L, D_MODEL), lambda i: (jnp.maximum(i * ATT_QB - 1, 0), col))

    return pl.pallas_call(
        functools.partial(_attn_kernel, blocks_per_seq=blocks_per_seq),
        grid=(n // tq,),
        in_specs=[cur(0), prev(1), cur(1), prev(2), cur(2), _const_spec((N_HEADS, 2 * L, L))],
        out_specs=[pl.BlockSpec((tq, D_MODEL), lambda i: (i, 0)), pl.BlockSpec((tq, LANE), lambda i: (i, 0))],
        out_shape=[jax.ShapeDtypeStruct((n, D_MODEL), bf16), jax.ShapeDtypeStruct((n, LANE), f32)],
        compiler_params=_cparams(("arbitrary",)),
        name="dilated_attn",
    )(qkv, qkv, qkv, qkv, qkv, bias)


def _split_bf16(v):
    hi = v.astype(bf16)
    return hi, (v - hi.astype(f32)).astype(bf16)


def _unstride_rows(ref, perm_ref, apply):
    dil, per, width = ref.shape[1:]
    blk = perm_ref.shape[0]
    piece = blk // dil
    outs = []
    for s in range(dil * per // blk):
        v = ref[0, :, s * piece:(s + 1) * piece, :].reshape(blk, width)
        outs.append(apply(perm_ref[...], v))
    return jnp.concatenate(outs, axis=0)


def _merge_kernel(*refs):
    ng = N_DIL_GROUPS
    o_refs, lse_refs, perm_refs = refs[:ng], refs[ng:2 * ng], refs[2 * ng:3 * ng - 1]
    (x_ref, e_ref, wo_ref, g_ref, b_ref, wrh_ref, wrl_ref, x3_ref, gate_ref, sel_ref) = refs[3 * ng - 1:]
    os_ = [o_refs[0][...].astype(f32)]
    ls = [lse_refs[0][...]]
    for o_ref, lse_ref, perm_ref in zip(o_refs[1:], lse_refs[1:], perm_refs):
        os_.append(_unstride_rows(o_ref, perm_ref, lambda p, v: _dot(p, v)))
        ls.append(_unstride_rows(lse_ref, perm_ref, _permute_rows_f32))
    mx = functools.reduce(jnp.maximum, ls)
    es = [jnp.exp(l - mx) for l in ls]
    den = functools.reduce(jnp.add, es)
    merged = None
    for g in range(ng):
        wfull = _dot((es[g] / den).astype(bf16), e_ref[...])
        term = wfull * os_[g]
        merged = term if merged is None else merged + term
    att = _dot(merged.astype(bf16), wo_ref[...])
    x3 = _layer_norm_rows(DEEPNORM_ALPHA * x_ref[...] + att, g_ref[...], b_ref[...])
    x3_ref[...] = x3

    xh, xl = _split_bf16(x3)
    logits = _dot(xh, wrh_ref[...]) + _dot(xl, wrh_ref[...]) + _dot(xh, wrl_ref[...])
    lane = lax.broadcasted_iota(jnp.int32, logits.shape, 1)
    neg = -jnp.inf
    logits = jnp.where(lane < N_EXPERTS, logits, neg)
    m1 = jnp.max(logits, axis=-1, keepdims=True)
    i1 = jnp.min(jnp.where(logits == m1, lane, LANE), axis=-1, keepdims=True)
    pick1 = lane == i1
    rest = jnp.where(pick1, neg, logits)
    m2 = jnp.max(rest, axis=-1, keepdims=True)
    i2 = jnp.min(jnp.where(rest == m2, lane, LANE), axis=-1, keepdims=True)
    pick2 = lane == i2
    e2 = jnp.exp(m2 - m1)
    gate_ref[...] = jnp.where(pick1, 1.0 / (1.0 + e2), jnp.where(pick2, e2 / (1.0 + e2), 0.0))
    sel_ref[...] = jnp.where(pick1 | pick2, 1.0, 0.0)


def _merge_layer(os_, lses, x, expand, w_o, ln_g, ln_b, wr_hi, wr_lo, batch):
    n = x.shape[0]
    tiles_per_seq = n // batch // MRG_TM
    row = lambda w: pl.BlockSpec((MRG_TM, w), lambda i: (i, 0))
    dils = [dil for _, dil in DILATED_PAIRS[1:]]
    perms = [jnp.asarray(_stride_perm(_perm_rows(dil), dil).T, bf16) for dil in dils]
    return pl.pallas_call(
        _merge_kernel,
        grid=(n // MRG_TM,),
        in_specs=[row(D_MODEL)] + [_strided_spec(MRG_TM, dil, D_MODEL, tiles_per_seq) for dil in dils]
                 + [row(LANE)] + [_strided_spec(MRG_TM, dil, LANE, tiles_per_seq) for dil in dils]
                 + [_const_spec(p.shape) for p in perms]
                 + [row(D_MODEL), _const_spec((LANE, D_MODEL)),
                    _const_spec((D_MODEL, D_MODEL)), _const_spec((1, D_MODEL)), _const_spec((1, D_MODEL)),
                    _const_spec((D_MODEL, LANE)), _const_spec((D_MODEL, LANE))],
        out_specs=[row(D_MODEL), row(LANE), row(LANE)],
        out_shape=[jax.ShapeDtypeStruct((n, D_MODEL), f32), jax.ShapeDtypeStruct((n, LANE), f32),
                   jax.ShapeDtypeStruct((n, LANE), f32)],
        compiler_params=_cparams(("arbitrary",)),
        name="attn_merge",
    )(*os_, *lses, *perms, x, expand, w_o, ln_g, ln_b, wr_hi, wr_lo)


def _row_copy(src_hbm, src_row, dst_ref, dst_row, sem):
    return pltpu.make_async_copy(src_hbm.at[pl.ds(src_row, 1), :], dst_ref.at[pl.ds(dst_row, 1), :], sem)


def _dispatch_kernel(pad_start_ref, pad_cnt_ref, ra_ref, rb_ref, x_ref, xs_hbm, zbuf, sem, zsem):
    def fill_copy(e, k):
        return _row_copy(zbuf, 0, xs_hbm, pad_start_ref[e] + k, zsem)

    def tail_copy(k):
        start = pl.multiple_of(pad_start_ref[N_EXPERTS] + k * MOE_TM, MOE_TM)
        return pltpu.make_async_copy(zbuf, xs_hbm.at[pl.ds(start, MOE_TM), :], zsem)

    @pl.when(pl.program_id(0) == 0)
    def _():
        zbuf[...] = jnp.zeros_like(zbuf)
        for e in range(N_EXPERTS):
            lax.fori_loop(0, pad_cnt_ref[e], lambda k, c: (fill_copy(e, k).start(), c)[1], 0)
        lax.fori_loop(0, pad_cnt_ref[N_EXPERTS], lambda k, c: (tail_copy(k).start(), c)[1], 0)

    def issue(r, carry):
        _row_copy(x_ref, r, xs_hbm, ra_ref[0, 0, r], sem).start()
        _row_copy(x_ref, r, xs_hbm, rb_ref[0, 0, r], sem).start()
        return carry

    lax.fori_loop(0, GATHER_TM, issue, 0, unroll=ISSUE_UNROLL)

    for _ in range(TOP_K):
        pltpu.make_async_copy(x_ref, xs_hbm.at[pl.ds(0, GATHER_TM), :], sem).wait()

    @pl.when(pl.program_id(0) == 0)
    def _():
        for e in range(N_EXPERTS):
            lax.fori_loop(0, pad_cnt_ref[e], lambda k, c: (fill_copy(e, k).wait(), c)[1], 0)
        lax.fori_loop(0, pad_cnt_ref[N_EXPERTS], lambda k, c: (tail_copy(k).wait(), c)[1], 0)


def _dispatch_rows(pad_start, pad_cnt, ra, rb, x, rows_total):
    n = x.shape[0]
    smem = pl.BlockSpec((1, 1, GATHER_TM), lambda i, ps, pc: (i, 0, 0), memory_space=pltpu.SMEM)
    grid_spec = pltpu.PrefetchScalarGridSpec(
        num_scalar_prefetch=2,
        grid=(n // GATHER_TM,),
        in_specs=[smem, smem, pl.BlockSpec((GATHER_TM, D_MODEL), lambda i, ps, pc: (i, 0))],
        out_specs=pl.BlockSpec(memory_space=pl.ANY),
        scratch_shapes=[pltpu.VMEM((MOE_TM, D_MODEL), x.dtype), pltpu.SemaphoreType.DMA(()),
                        pltpu.SemaphoreType.DMA(())],
    )
    return pl.pallas_call(
        _dispatch_kernel,
        grid_spec=grid_spec,
        out_shape=jax.ShapeDtypeStruct((rows_total, D_MODEL), x.dtype),
        compiler_params=_cparams(("arbitrary",)),
        name="moe_dispatch",
    )(pad_start, pad_cnt, ra, rb, x)


def _moe_kernel(te_ref, act_ref, x_ref, w1_ref, w3_ref, w2_ref, o_ref):
    del te_ref
    i, f = pl.program_id(0), pl.program_id(1)

    @pl.when(act_ref[i] == 1)
    def _():
        xb = x_ref[...].astype(bf16)
        h = jax.nn.silu(_dot(xb, w1_ref[0])) * _dot(xb, w3_ref[0])
        part = _dot(h.astype(bf16), w2_ref[0])

        @pl.when(f == 0)
        def _():
            o_ref[...] = part

        @pl.when(f > 0)
        def _():
            o_ref[...] = o_ref[...] + part

    @pl.when((act_ref[i] == 0) & (f == 0))
    def _():
        o_ref[...] = jnp.zeros_like(o_ref)


def _moe_gmm(tile_expert, tile_active, xs, w1, w3, w2):
    n_tiles = tile_expert.shape[0]
    d_ff = w1.shape[2]
    grid_spec = pltpu.PrefetchScalarGridSpec(
        num_scalar_prefetch=2,
        grid=(n_tiles, d_ff // MOE_TF),
        in_specs=[pl.BlockSpec((MOE_TM, D_MODEL), lambda i, f, te, act: (i * act[i], 0)),
                  pl.BlockSpec((1, D_MODEL, MOE_TF), lambda i, f, te, act: (te[i], 0, f)),
                  pl.BlockSpec((1, D_MODEL, MOE_TF), lambda i, f, te, act: (te[i], 0, f)),
                  pl.BlockSpec((1, MOE_TF, D_MODEL), lambda i, f, te, act: (te[i], f, 0))],
        out_specs=pl.BlockSpec((MOE_TM, D_MODEL), lambda i, f, te, act: (i, 0)),
    )
    return pl.pallas_call(
        _moe_kernel,
        grid_spec=grid_spec,
        out_shape=jax.ShapeDtypeStruct((n_tiles * MOE_TM, D_MODEL), f32),
        compiler_params=_cparams(("arbitrary", "arbitrary")),
        name="moe_gmm",
    )(tile_expert, tile_active, xs, w1, w3, w2)


def _combine_kernel(ra_ref, rb_ref, ra_next, rb_next, ga_ref, gb_ref, x_ref, y_hbm, g_ref, b_ref, o_ref,
                    ybuf, sem):
    i = pl.program_id(0)
    slot = lax.rem(i, 2)

    def gather(ia_ref, ib_ref, s):
        def issue(r, carry):
            _row_copy(y_hbm, ia_ref[0, 0, r], ybuf.at[s, 0], r, sem.at[s, 0]).start()
            _row_copy(y_hbm, ib_ref[0, 0, r], ybuf.at[s, 1], r, sem.at[s, 1]).start()
            return carry

        lax.fori_loop(0, COMBINE_TM, issue, 0, unroll=ISSUE_UNROLL)

    @pl.when(i == 0)
    def _():
        gather(ra_ref, rb_ref, 0)

    @pl.when(i + 1 < pl.num_programs(0))
    def _():
        gather(ra_next, rb_next, 1 - slot)

    for k in range(TOP_K):
        pltpu.make_async_copy(y_hbm.at[pl.ds(0, COMBINE_TM), :], ybuf.at[slot, k], sem.at[slot, k]).wait()
    f = ga_ref[...] * ybuf[slot, 0] + gb_ref[...] * ybuf[slot, 1]
    o_ref[...] = _layer_norm_rows(DEEPNORM_ALPHA * x_ref[...] + f, g_ref[...], b_ref[...])


def _combine(ra, rb, ga, gb, x, ys, ln_g, ln_b):
    n = x.shape[0]
    last = n // COMBINE_TM - 1
    smem = pl.BlockSpec((1, 1, COMBINE_TM), lambda i: (i, 0, 0), memory_space=pltpu.SMEM)
    smem_next = pl.BlockSpec((1, 1, COMBINE_TM), lambda i: (jnp.minimum(i + 1, last), 0, 0),
                             memory_space=pltpu.SMEM)
    col = pl.BlockSpec((COMBINE_TM, 1), lambda i: (i, 0))
    row = pl.BlockSpec((COMBINE_TM, D_MODEL), lambda i: (i, 0))
    return pl.pallas_call(
        _combine_kernel,
        grid=(n // COMBINE_TM,),
        in_specs=[smem, smem, smem_next, smem_next, col, col, row, pl.BlockSpec(memory_space=pl.ANY),
                  _const_spec((1, D_MODEL)), _const_spec((1, D_MODEL))],
        out_specs=row,
        out_shape=jax.ShapeDtypeStruct((n, D_MODEL), f32),
        scratch_shapes=[pltpu.VMEM((2, TOP_K, COMBINE_TM, D_MODEL), f32), pltpu.SemaphoreType.DMA((2, TOP_K))],
        compiler_params=_cparams(("arbitrary",)),
        name="moe_combine",
    )(ra, rb, ra, rb, ga, gb, x, ys, ln_g, ln_b)


def _moe_plan(sel, gates):
    n = sel.shape[0]
    rows_total = TOP_K * n + N_EXPERTS * MOE_TM
    n_tiles = rows_total // MOE_TM
    seli = sel.astype(jnp.int32)
    cnt = jnp.sum(seli, axis=0)
    pos = jnp.cumsum(seli, axis=0) - seli
    padded = ((cnt + MOE_TM - 1) // MOE_TM) * MOE_TM
    bounds = jnp.cumsum(padded)
    row = (bounds - padded)[None, :] + pos
    pad_start = jnp.concatenate([bounds - padded + cnt, bounds[-1:]]).astype(jnp.int32)
    pad_cnt = jnp.concatenate([padded - cnt, (rows_total - bounds[-1:]) // MOE_TM]).astype(jnp.int32)
    tile_start = jnp.arange(n_tiles, dtype=jnp.int32) * MOE_TM
    last_used = jnp.max(jnp.where(cnt > 0, jnp.arange(N_EXPERTS, dtype=jnp.int32), 0))
    tile_expert = jnp.sum((tile_start[:, None] >= bounds[None, :]).astype(jnp.int32), axis=1)
    tile_expert = jnp.minimum(tile_expert, last_used)
    tile_active = (tile_start < bounds[-1]).astype(jnp.int32)
    eid = jnp.arange(N_EXPERTS, dtype=jnp.int32)[None, :]
    is_a = eid == jnp.min(jnp.where(sel, eid, N_EXPERTS), axis=1, keepdims=True)
    is_b = eid == jnp.max(jnp.where(sel, eid, -1), axis=1, keepdims=True)
    pick = lambda a, m: jnp.sum(jnp.where(m, a, 0), axis=1)
    return (pad_start, pad_cnt, tile_expert, tile_active,
            pick(row, is_a).astype(jnp.int32), pick(row, is_b).astype(jnp.int32),
            pick(gates, is_a)[:, None], pick(gates, is_b)[:, None])


def kernel(x, s5_w_in, s5_a_re, s5_a_im, s5_log_dt, s5_b_re, s5_b_im, s5_c_re, s5_c_im, s5_d, s5_w_glu,
           attn_w_qkv, attn_w_o, rel_bias, ffn_w1, ffn_w3, ffn_w2, moe_w_router, moe_w1, moe_w3, moe_w2,
           ln_g, ln_b):
    batch, seq, d = x.shape
    n = batch * seq
    assert d == D_MODEL
    assert seq % (DILATED_PAIRS[-1][1] * ATTN_BLOCK * ATT_QB) == 0
    assert seq % S5_TB == 0 and seq % FFN_TM == 0 and FFN_TM == MRG_TM
    vec = lambda v: v.reshape(1, D_MODEL).astype(f32)

    bre, bim, cre, cimn, tab = _s5_tables(s5_a_re[0], s5_a_im[0], s5_log_dt[0], s5_b_re[0], s5_b_im[0],
                                          s5_c_re[0], s5_c_im[0])
    x1 = _s5_layer(x.reshape(n, d), s5_w_in[0].astype(bf16), bre, bim, cre, cimn, tab, vec(s5_d[0]),
                   s5_w_glu[0].astype(bf16), vec(ln_g[0, 0]), vec(ln_b[0, 0]), batch)
    x2, *xgs = _ffn_layer(x1, ffn_w1[0].astype(bf16), ffn_w3[0].astype(bf16), ffn_w2[0].astype(bf16),
                          vec(ln_g[0, 1]), vec(ln_b[0, 1]), batch)

    wq = attn_w_qkv[0].astype(bf16)
    bias = _attn_bias(rel_bias)
    os_, lses = [], []
    for g, (_, dil) in enumerate(DILATED_PAIRS):
        qkv = _qkv_proj(xgs[g].reshape(n, d), wq, g)
        o, lse = _attention(qkv, bias[g], seq // (dil * ATTN_BLOCK))
        os_.append(o if g == 0 else o.reshape(batch, dil, seq // dil, d))
        lses.append(lse if g == 0 else lse.reshape(batch, dil, seq // dil, LANE))
    expand = (jnp.arange(LANE)[:, None] == (jnp.arange(D_MODEL)[None, :] // HEAD_DIM)).astype(bf16)
    wr = jnp.pad(moe_w_router[0].astype(f32), ((0, 0), (0, LANE - N_EXPERTS)))
    wr_hi, wr_lo = _split_bf16(wr)
    x3, gates, sel = _merge_layer(os_, lses, x2, expand, attn_w_o[0].astype(bf16), vec(ln_g[1, 0]),
                                  vec(ln_b[1, 0]), wr_hi, wr_lo, batch)

    pad_start, pad_cnt, tile_expert, tile_active, ra, rb, ga, gb = _moe_plan(sel[:, :N_EXPERTS] > 0.5,
                                                                             gates[:, :N_EXPERTS])
    xs = _dispatch_rows(pad_start, pad_cnt, ra.reshape(-1, 1, GATHER_TM), rb.reshape(-1, 1, GATHER_TM), x3,
                        TOP_K * n + N_EXPERTS * MOE_TM)
    ys = _moe_gmm(tile_expert, tile_active, xs, moe_w1[0].astype(bf16), moe_w3[0].astype(bf16),
                  moe_w2[0].astype(bf16))
    out = _combine(ra.reshape(-1, 1, COMBINE_TM), rb.reshape(-1, 1, COMBINE_TM), ga, gb, x3, ys,
                   vec(ln_g[1, 1]), vec(ln_b[1, 1]))
    return out.reshape(batch, seq, d)
```
